```python
import math
import jax, jax.numpy as jnp
from jax import lax
import numpy as np

D_MODEL = 2048
BATCH = 2
SEQ = 16384
DEPTH = 2

ALPHA = (2 * DEPTH) ** 0.25
BETA = (8 * DEPTH) ** -0.25
LN_EPS = 1e-5
ROPE_THETA = 10000.0
BLOCK = 128

A_HEADS = 16
A_KV_HEADS = 2
A_HEAD_DIM = 64
WINDOW = 128
A_Q = A_HEADS * A_HEAD_DIM
A_KV = A_KV_HEADS * A_HEAD_DIM
A_IN = A_Q + 2 * A_KV

B_HEADS = 16
B_HEAD_DIM = 64
B_W = B_HEADS * B_HEAD_DIM
DECAY_LORA = 96
AAA_LORA = 96
GATE_LORA = 256
GN_EPS = 64e-5
B_IN = 3 * B_W + DECAY_LORA + AAA_LORA + GATE_LORA
EVEN_IN = A_IN + B_IN
EVEN_MIX = A_Q + B_W

C_HEADS = 16
C_KV_HEADS = 4
C_HEAD_DIM = 128
IDX_HEADS = 16
IDX_DIM = 64
TOPK_MAX = 256
ODD_SPLITS = (C_HEADS * C_HEAD_DIM, C_KV_HEADS * C_HEAD_DIM, C_KV_HEADS * C_HEAD_DIM,
              IDX_HEADS * IDX_DIM, IDX_DIM, IDX_HEADS)
ODD_IN = sum(ODD_SPLITS)
ODD_MIX = C_HEADS * C_HEAD_DIM

PEER_HEADS = 8
PEER_NKEYS = 128
PEER_N = PEER_NKEYS * PEER_NKEYS
PEER_QDIM = 256
PEER_TOPK = 16

kernel_name = "hybrid_swa_rwkv7_dsa_peer_block"


def split_cols(z, sizes):
    return jnp.split(z, np.cumsum(sizes)[:-1].tolist(), axis=-1)


def layer_norm(x, g, b):
    xf = x.astype(jnp.float32)
    mu = jnp.mean(xf, -1, keepdims=True)
    var = jnp.mean(jnp.square(xf - mu), -1, keepdims=True)
    y = (xf - mu) * lax.rsqrt(var + LN_EPS)
    return (y * g.astype(jnp.float32) + b.astype(jnp.float32)).astype(x.dtype)


def modulation(c_act, w, b):
    m = c_act @ w + b
    shift, scale, gate = jnp.split(m[:, None, :], 3, axis=-1)
    return shift, scale, gate


def rotary(t, positions):
    d = t.shape[-1]
    half = d // 2
    inv = ROPE_THETA ** (-jnp.arange(half, dtype=jnp.float32) * (2.0 / d))
    ang = positions.astype(jnp.float32)[..., None] * inv
    cos, sin = jnp.cos(ang)[:, :, None, :], jnp.sin(ang)[:, :, None, :]
    tf = t.astype(jnp.float32)
    t1, t2 = tf[..., :half], tf[..., half:]
    return jnp.concatenate([t1 * cos - t2 * sin, t2 * cos + t1 * sin], -1).astype(t.dtype)


def token_shift(z):
    return jnp.pad(z[:, :-1], ((0, 0), (1, 0), (0, 0)))


def swa_sinks(q, k, v, sinks):
    B_, S_, H, dh = q.shape
    nb = S_ // BLOCK
    G = H // A_KV_HEADS
    qb = q.reshape(B_, nb, BLOCK, A_KV_HEADS, G, dh)

    def with_prev(t):
        tb = t.reshape(B_, nb, BLOCK, A_KV_HEADS, dh)
        prev = jnp.pad(tb[:, :-1], ((0, 0), (1, 0), (0, 0), (0, 0), (0, 0)))
        return jnp.concatenate([prev, tb], axis=2)

    kc, vc = with_prev(k), with_prev(v)
    s = jnp.einsum('bnqkgd,bnskd->bnkgqs', qb, kc).astype(jnp.float32) * (dh ** -0.5)
    qi = jnp.arange(BLOCK)[:, None] + BLOCK
    kj = jnp.arange(2 * BLOCK)[None, :]
    dist = qi - kj
    band = (dist >= 0) & (dist < WINDOW)
    has_prev = (jnp.arange(nb)[:, None, None] > 0) | (kj[None] >= BLOCK)
    mask = band[None] & has_prev
    s = jnp.where(mask[None, :, None, None], s, -jnp.inf)
    sink = jnp.broadcast_to(sinks.astype(jnp.float32).reshape(1, 1, A_KV_HEADS, G, 1, 1),
                            s.shape[:-1] + (1,))
    p = jax.nn.softmax(jnp.concatenate([s, sink], -1), axis=-1)[..., :-1]
    o = jnp.einsum('bnkgqs,bnskd->bnqkgd', p.astype(v.dtype), vc)
    return o.reshape(B_, S_, H * dh)


def rwkv7(z, w0, w2, a0, a2, g2, k_k, k_a, r_k, lnx_g, lnx_b):
    out_dtype = z.dtype
    f32 = jnp.float32
    z = z.astype(f32)
    r, k, v, wd, ad, gd = split_cols(z, (B_W, B_W, B_W, DECAY_LORA, AAA_LORA, GATE_LORA))
    B_, S_, _ = r.shape
    heads = lambda t: t.reshape(B_, S_, B_HEADS, B_HEAD_DIM)
    w_log = -jax.nn.softplus(-(w0.astype(f32) + jnp.tanh(wd) @ w2.astype(f32))) - 0.5
    decay = jnp.exp(-jnp.exp(w_log))
    a = jax.nn.sigmoid(a0.astype(f32) + ad @ a2.astype(f32))
    g = jax.nn.sigmoid(gd) @ g2.astype(f32)
    kk = heads(k * k_k.astype(f32))
    kk = kk / jnp.maximum(jnp.sqrt(jnp.sum(kk * kk, -1, keepdims=True)), 1e-12)
    k = k * (1.0 + (a - 1.0) * k_a.astype(f32))
    rh, kh, vh, ah, wh = heads(r), heads(k), heads(v), heads(a), heads(decay)

    def step(state, inp):
        r_t, w_t, k_t, v_t, kk_t, a_t = inp
        sa = jnp.einsum('bhvk,bhk->bhv', state, -kk_t)
        state = (state * w_t[:, :, None, :] + sa[..., None] * (kk_t * a_t)[:, :, None, :]
                 + v_t[..., None] * k_t[:, :, None, :])
        y_t = jnp.einsum('bhvk,bhk->bhv', state, r_t)
        return state, y_t

    s0 = jnp.zeros((B_, B_HEADS, B_HEAD_DIM, B_HEAD_DIM), f32)
    xs = tuple(jnp.moveaxis(t, 1, 0) for t in (rh, wh, kh, vh, kk, ah))
    _, y = lax.scan(step, s0, xs)
    y = jnp.moveaxis(y, 0, 1)
    mu = jnp.mean(y, -1, keepdims=True)
    var = jnp.mean(jnp.square(y - mu), -1, keepdims=True)
    y = ((y - mu) * lax.rsqrt(var + GN_EPS)).reshape(B_, S_, B_W)
    y = y * lnx_g.astype(f32) + lnx_b.astype(f32)
    bonus = jnp.sum(rh * kh * r_k.astype(f32), -1, keepdims=True) * vh
    y = (y + bonus.reshape(B_, S_, B_W)) * g
    return y.astype(out_dtype)


def dsa_attention(q, k, v, qi, ki, wi):
    B_, S_, H, dh = q.shape
    G = H // C_KV_HEADS
    nb = S_ // BLOCK
    n_sel = min(TOPK_MAX, S_ // 4)
    wi = wi.astype(jnp.float32) * (IDX_HEADS ** -0.5) * (IDX_DIM ** -0.5)
    key_pos = jnp.arange(S_)

    def blocks(t):
        return jnp.moveaxis(t.reshape((B_, nb, BLOCK) + t.shape[2:]), 1, 0)

    def one_block(args):
        qb, qib, wib, start = args
        tpos = start + jnp.arange(BLOCK)
        sc = jax.nn.relu(jnp.einsum('bthd,bsd->bths', qib, ki).astype(jnp.float32))
        score = jnp.einsum('bths,bth->bts', sc, wib)
        causal = key_pos[None, :] <= tpos[:, None]
        score = jnp.where(causal[None], score, -jnp.inf)
        _, idx = lax.top_k(score, n_sel)
        valid = idx <= tpos[None, :, None]
        kg = jax.vmap(lambda kb_, ib_: kb_[ib_])(k, idx)
        vg = jax.vmap(lambda vb_, ib_: vb_[ib_])(v, idx)
        qg = qb.reshape(B_, BLOCK, C_KV_HEADS, G, dh)
        s = jnp.einsum('btkgd,btskd->btkgs', qg, kg).astype(jnp.float32) * (dh ** -0.5)
        s = jnp.where(valid[:, :, None, None, :], s, -jnp.inf)
        p = jax.nn.softmax(s, axis=-1)
        o = jnp.einsum('btkgs,btskd->btkgd', p.astype(v.dtype), vg)
        return o.reshape(B_, BLOCK, H * dh)

    starts = jnp.arange(nb) * BLOCK
    out = lax.map(one_block, (blocks(q), blocks(qi), blocks(wi), starts))
    return jnp.moveaxis(out, 0, 1).reshape(B_, S_, H * dh)


def peer(h, wq, subkeys, u_tab, v_tab):
    B_, S_, D = h.shape
    tok = h.reshape(-1, BLOCK, D)

    def one_chunk(xc):
        q = (xc @ wq).reshape(BLOCK, PEER_HEADS, 2, PEER_QDIM // 2)
        s = jnp.einsum('thpd,phnd->thpn', q, subkeys).astype(jnp.float32)
        v_top, i_top = lax.top_k(s, PEER_TOPK)
        cand = v_top[:, :, 0, :, None] + v_top[:, :, 1, None, :]
        best, ci = lax.top_k(cand.reshape(BLOCK, PEER_HEADS, PEER_TOPK * PEER_TOPK), PEER_TOPK)
        i1 = jnp.take_along_axis(i_top[:, :, 0], ci // PEER_TOPK, axis=-1)
        i2 = jnp.take_along_axis(i_top[:, :, 1], ci % PEER_TOPK, axis=-1)
        eid = i1 * PEER_NKEYS + i2
        gsm = jax.nn.softmax(best, axis=-1)
        u = u_tab[eid]
        act = jax.nn.gelu(jnp.einsum('thkd,td->thk', u, xc).astype(jnp.float32),
                          approximate=False) * gsm
        return jnp.einsum('thk,thkd->td', act.astype(xc.dtype), v_tab[eid])

    return lax.map(one_chunk, tok).reshape(B_, S_, D)


def even_mixer(h, positions, w_in, w_out, sinks, mu, w0, w2, a0, a2, g2, k_k, k_a, r_k, lnx_g, lnx_b):
    B_, S_, _ = h.shape
    z = h @ w_in
    za, zb = z[..., :A_IN], z[..., A_IN:]
    qa, ka, va = split_cols(za, (A_Q, A_KV, A_KV))
    qa = rotary(qa.reshape(B_, S_, A_HEADS, A_HEAD_DIM), positions)
    ka = rotary(ka.reshape(B_, S_, A_KV_HEADS, A_HEAD_DIM), positions)
    va = va.reshape(B_, S_, A_KV_HEADS, A_HEAD_DIM)
    ya = swa_sinks(qa, ka, va, sinks)
    zb = zb + (token_shift(zb) - zb) * mu
    yb = rwkv7(zb, w0, w2, a0, a2, g2, k_k, k_a, r_k, lnx_g, lnx_b)
    return jnp.concatenate([ya, yb], axis=-1) @ w_out


def odd_mixer(h, positions, w_in, w_out):
    B_, S_, _ = h.shape
    z = h @ w_in
    q, k, v, qi, ki, wi = split_cols(z, ODD_SPLITS)
    q = rotary(q.reshape(B_, S_, C_HEADS, C_HEAD_DIM), positions)
    k = rotary(k.reshape(B_, S_, C_KV_HEADS, C_HEAD_DIM), positions)
    v = v.reshape(B_, S_, C_KV_HEADS, C_HEAD_DIM)
    qi = rotary(qi.reshape(B_, S_, IDX_HEADS, IDX_DIM), positions)
    ki = rotary(ki.reshape(B_, S_, 1, IDX_DIM), positions)[:, :, 0]
    return dsa_attention(q, k, v, qi, ki, wi) @ w_out


def setup_inputs(seed: int = 0) -> dict:
    key = jax.random.key(seed)
    ks = jax.random.split(key, 32)
    f32 = jnp.float32
    NE, NO = (DEPTH + 1) // 2, DEPTH // 2

    def nrm(k, shape, scale):
        return jax.random.normal(k, shape, f32) * scale

    x = nrm(ks[0], (BATCH, SEQ, D_MODEL), 1.0)
    c = nrm(ks[1], (BATCH, D_MODEL), 1.0)
    positions = (jnp.arange(SEQ, dtype=jnp.int32)[None, :]
                 + jax.random.randint(ks[2], (BATCH, 1), 0, 4096, dtype=jnp.int32))
    ev_w_in = nrm(ks[3], (NE, D_MODEL, EVEN_IN), D_MODEL ** -0.5)
    ev_w_out = nrm(ks[4], (NE, EVEN_MIX, D_MODEL), BETA * EVEN_MIX ** -0.5)
    ev_sinks = nrm(ks[5], (NE, A_HEADS), 0.5)
    ev_mu = jax.random.uniform(ks[6], (NE, B_IN), f32, 0.0, 1.0)
    ev_w0 = jax.random.uniform(ks[7], (NE, B_W), f32, -4.0, 1.0)
    ev_w2 = nrm(ks[8], (NE, DECAY_LORA, B_W), 0.5 * DECAY_LORA ** -0.5)
    ev_a0 = nrm(ks[9], (NE, B_W), 0.1)
    ev_a2 = nrm(ks[10], (NE, AAA_LORA, B_W), 0.5 * AAA_LORA ** -0.5)
    ev_g2 = nrm(ks[11], (NE, GATE_LORA, B_W), GATE_LORA ** -0.5)
    ev_k_k = 0.85 + nrm(ks[12], (NE, B_W), 0.02)
    ev_k_a = 1.0 + nrm(ks[13], (NE, B_W), 0.02)
    ev_r_k = nrm(ks[14], (NE, B_HEADS, B_HEAD_DIM), 0.1)
    ev_lnx_g = 1.0 + nrm(ks[15], (NE, B_W), 0.02)
    ev_lnx_b = nrm(ks[16], (NE, B_W), 0.02)
    od_w_in = nrm(ks[17], (NO, D_MODEL, ODD_IN), D_MODEL ** -0.5)
    od_w_out = nrm(ks[18], (NO, ODD_MIX, D_MODEL), BETA * ODD_MIX ** -0.5)
    mod_w = nrm(ks[19], (DEPTH, 2, D_MODEL, 3 * D_MODEL), 0.1 * D_MODEL ** -0.5)
    mod_b = nrm(ks[20], (DEPTH, 2, 3 * D_MODEL), 0.02)
    ln_g = 1.0 + nrm(ks[21], (DEPTH, 2, D_MODEL), 0.02)
    ln_b = nrm(ks[22], (DEPTH, 2, D_MODEL), 0.02)
    peer_wq = nrm(ks[23], (DEPTH, D_MODEL, PEER_HEADS * PEER_QDIM), D_MODEL ** -0.5)
    peer_subkeys = nrm(ks[24], (DEPTH, 2, PEER_HEADS, PEER_NKEYS, PEER_QDIM // 2),
                       (PEER_QDIM // 2) ** -0.5)
    peer_u = nrm(ks[25], (DEPTH, PEER_N, D_MODEL), D_MODEL ** -0.5)
    peer_v = nrm(ks[26], (DEPTH, PEER_N, D_MODEL), BETA * PEER_TOPK ** -0.5)
    return {"x": x, "c": c, "positions": positions,
            "ev_w_in": ev_w_in, "ev_w_out": ev_w_out, "ev_sinks": ev_sinks, "ev_mu": ev_mu,
            "ev_w0": ev_w0, "ev_w2": ev_w2, "ev_a0": ev_a0, "ev_a2": ev_a2, "ev_g2": ev_g2,
            "ev_k_k": ev_k_k, "ev_k_a": ev_k_a, "ev_r_k": ev_r_k,
            "ev_lnx_g": ev_lnx_g, "ev_lnx_b": ev_lnx_b,
            "od_w_in": od_w_in, "od_w_out": od_w_out,
            "mod_w": mod_w, "mod_b": mod_b, "ln_g": ln_g, "ln_b": ln_b,
            "peer_wq": peer_wq, "peer_subkeys": peer_subkeys, "peer_u": peer_u, "peer_v": peer_v}


def reference(x, c, positions, ev_w_in, ev_w_out, ev_sinks, ev_mu, ev_w0, ev_w2, ev_a0, ev_a2,
              ev_g2, ev_k_k, ev_k_a, ev_r_k, ev_lnx_g, ev_lnx_b, od_w_in, od_w_out,
              mod_w, mod_b, ln_g, ln_b, peer_wq, peer_subkeys, peer_u, peer_v):
    c_act = jax.nn.silu(c)
    for layer in range(DEPTH):
        j = layer // 2
        shift, scale, gate = modulation(c_act, mod_w[layer, 0], mod_b[layer, 0])
        h = x * (1.0 + scale) + shift
        if layer % 2 == 0:
            y = even_mixer(h, positions, ev_w_in[j], ev_w_out[j], ev_sinks[j], ev_mu[j],
                           ev_w0[j], ev_w2[j], ev_a0[j], ev_a2[j], ev_g2[j], ev_k_k[j],
                           ev_k_a[j], ev_r_k[j], ev_lnx_g[j], ev_lnx_b[j])
        else:
            y = odd_mixer(h, positions, od_w_in[j], od_w_out[j])
        x = layer_norm(ALPHA * x + (1.0 + gate) * y, ln_g[layer, 0], ln_b[layer, 0])
        shift, scale, gate = modulation(c_act, mod_w[layer, 1], mod_b[layer, 1])
        h = x * (1.0 + scale) + shift
        y = peer(h, peer_wq[layer], peer_subkeys[layer], peer_u[layer], peer_v[layer])
        x = layer_norm(ALPHA * x + (1.0 + gate) * y, ln_g[layer, 1], ln_b[layer, 1])
    return x
```

```python
import functools
import math

import jax
import jax.numpy as jnp
from jax import lax
from jax.experimental import pallas as pl
from jax.experimental.pallas import tpu as pltpu

F32 = jnp.float32
BF16 = jnp.bfloat16

LANES = 128
DEPTH = 2
ALPHA = (2 * DEPTH) ** 0.25
LN_EPS = 1e-5
ROPE_THETA = 10000.0
BLOCK = 128
A_HEADS, A_KV_HEADS, A_HEAD_DIM = 16, 2, 64
B_HEADS, B_HEAD_DIM = 16, 64
GN_EPS = 64e-5
LORA_PAD = 128
C_HEADS, C_KV_HEADS, C_HEAD_DIM = 16, 4, 128
IDX_HEADS, IDX_DIM = 16, 64
TOPK_MAX = 256
PEER_HEADS, PEER_NKEYS, PEER_TOPK = 8, 128, 16
NEG_BIG = -1e30
VMEM_LIMIT = 56 * 1024 * 1024


def _cparams(sem):
    return pltpu.CompilerParams(dimension_semantics=sem, vmem_limit_bytes=VMEM_LIMIT)


def _tile(n, pref, unit=LANES):
    if n <= pref:
        return n
    t = (pref // unit) * unit
    while n % t:
        t -= unit
    assert t > 0, (n, pref)
    return t


def _dot(a, b):
    return jnp.dot(a, b, preferred_element_type=F32)


def _dot_nt(a, b):
    return lax.dot_general(a, b, (((1,), (1,)), ((), ())), preferred_element_type=F32)


def _dot_tn(a, b):
    return lax.dot_general(a, b, (((0,), (0,)), ((), ())), preferred_element_type=F32)


def _split_dot(a, b_bf16):
    hi = a.astype(BF16)
    lo = (a - hi.astype(F32)).astype(BF16)
    return _dot(hi, b_bf16) + _dot(lo, b_bf16)


def _layer_norm(v, g, b):
    mu = jnp.mean(v, axis=-1, keepdims=True)
    d = v - mu
    var = jnp.mean(d * d, axis=-1, keepdims=True)
    return d * lax.rsqrt(var + LN_EPS) * g + b


def _mod_kernel(c_ref, w_ref, b_ref, o_ref):
    c = c_ref[...]
    ca = c * (1.0 / (1.0 + jnp.exp(-c)))
    o_ref[0] = _split_dot(ca, w_ref[0].astype(BF16)) + _split_dot(
        ca, (w_ref[0] - w_ref[0].astype(BF16).astype(F32)).astype(BF16)) + b_ref[0]


def _modulation(c, mod_w, mod_b):
    nb, d = c.shape
    nl = mod_w.shape[0] * mod_w.shape[1]
    n3 = mod_w.shape[-1]
    rows = 8
    cp = jnp.zeros((rows, d), F32).at[:nb].set(c)
    w = mod_w.reshape(nl, d, n3)
    b = mod_b.reshape(nl, 1, n3)
    tn = _tile(n3, 512)
    out = pl.pallas_call(
        _mod_kernel,
        out_shape=jax.ShapeDtypeStruct((nl, rows, n3), F32),
        grid=(nl, n3 // tn),
        in_specs=[pl.BlockSpec((rows, d), lambda l, j: (0, 0)),
                  pl.BlockSpec((1, d, tn), lambda l, j: (l, 0, j)),
                  pl.BlockSpec((1, 1, tn), lambda l, j: (l, 0, j))],
        out_specs=pl.BlockSpec((1, rows, tn), lambda l, j: (l, 0, j)),
        compiler_params=_cparams(("parallel", "parallel")),
        name="modulation",
    )(cp, w, b)
    return out[:, :nb]


def _proj_kernel(*refs, rot_half):
    if rot_half:
        x_ref, sc_ref, sh_ref, w_ref, cos_ref, sin_ref, o_ref, h_scr = refs
    else:
        x_ref, sc_ref, sh_ref, w_ref, o_ref, h_scr = refs

    @pl.when(pl.program_id(2) == 0)
    def _():
        h_scr[...] = (x_ref[0] * (1.0 + sc_ref[0]) + sh_ref[0]).astype(BF16)

    z = _dot(h_scr[...], w_ref[...])
    if rot_half:
        tn = z.shape[1]
        lane = lax.broadcasted_iota(jnp.int32, z.shape, 1)
        first = (lane % (2 * rot_half)) < rot_half
        swapped = jnp.where(first, pltpu.roll(z, tn - rot_half, 1), pltpu.roll(z, rot_half, 1))
        reps = tn // LANES
        cos = jnp.concatenate([cos_ref[0]] * reps, axis=1)
        sin = jnp.concatenate([sin_ref[0]] * reps, axis=1)
        z = z * cos + swapped * sin
    o_ref[0] = z.astype(o_ref.dtype)


def _project(x, scale, shift, w, *, out_dtype, tn, rot=None, tm=512):
    nb, s, d = x.shape
    n = w.shape[1]
    tm = _tile(s, tm)
    tn = _tile(n, tn)
    in_specs = [pl.BlockSpec((1, tm, d), lambda b, i, j: (b, i, 0)),
                pl.BlockSpec((1, 1, d), lambda b, i, j: (b, 0, 0)),
                pl.BlockSpec((1, 1, d), lambda b, i, j: (b, 0, 0)),
                pl.BlockSpec((d, tn), lambda b, i, j: (0, j))]
    args = [x, scale, shift, w]
    rot_half = 0
    if rot is not None:
        rot_half, cos, sin = rot
        in_specs += [pl.BlockSpec((1, tm, LANES), lambda b, i, j: (b, i, 0))] * 2
        args += [cos, sin]
    return pl.pallas_call(
        functools.partial(_proj_kernel, rot_half=rot_half),
        out_shape=jax.ShapeDtypeStruct((nb, s, n), out_dtype),
        grid=(nb, s // tm, n // tn),
        in_specs=in_specs,
        out_specs=pl.BlockSpec((1, tm, tn), lambda b, i, j: (b, i, j)),
        scratch_shapes=[pltpu.VMEM((tm, d), BF16)],
        compiler_params=_cparams(("parallel", "parallel", "arbitrary")),
        name="in_proj_rot%d" % rot_half,
    )(*args)


def _rot_tables(positions, rot_half):
    inv = ROPE_THETA ** (-jnp.arange(rot_half, dtype=F32) * (2.0 / (2 * rot_half)))
    ang = positions.astype(F32)[..., None] * inv
    cos, sin = jnp.cos(ang), jnp.sin(ang)
    reps = LANES // (2 * rot_half)
    cos_t = jnp.concatenate([cos, cos] * reps, axis=-1)
    sin_t = jnp.concatenate([-sin, sin] * reps, axis=-1)
    return cos_t, sin_t


def _outproj_kernel(*refs, n_in):
    ys = refs[:n_in]
    ws = refs[n_in:2 * n_in]
    x_ref, gate_ref, g_ref, b_ref, o_ref = refs[2 * n_in:]
    y = _dot(ys[0][0], ws[0][...])
    for y_ref, w_ref in zip(ys[1:], ws[1:]):
        y = y + _dot(y_ref[0], w_ref[...])
    v = ALPHA * x_ref[0] + (1.0 + gate_ref[0]) * y
    o_ref[0] = _layer_norm(v, g_ref[...], b_ref[...])


def _outproj_ln(ys, ws, x, gate, ln_g, ln_b, *, tm=256):
    nb, s, d = x.shape
    tm = _tile(s, tm)
    n_in = len(ys)
    in_specs = ([pl.BlockSpec((1, tm, y.shape[2]), lambda b, i: (b, i, 0)) for y in ys]
                + [pl.BlockSpec(w.shape, lambda b, i: (0, 0)) for w in ws]
                + [pl.BlockSpec((1, tm, d), lambda b, i: (b, i, 0)),
                   pl.BlockSpec((1, 1, d), lambda b, i: (b, 0, 0)),
                   pl.BlockSpec((1, d), lambda b, i: (0, 0)),
                   pl.BlockSpec((1, d), lambda b, i: (0, 0))])
    return pl.pallas_call(
        functools.partial(_outproj_kernel, n_in=n_in),
        out_shape=jax.ShapeDtypeStruct((nb, s, d), F32),
        grid=(nb, s // tm),
        in_specs=in_specs,
        out_specs=pl.BlockSpec((1, tm, d), lambda b, i: (b, i, 0)),
        compiler_params=_cparams(("parallel", "parallel")),
        name="out_proj_ln",
    )(*ys, *ws, x, gate, ln_g.reshape(1, d), ln_b.reshape(1, d))


def _swa_kernel(sink_ref, q_ref, kp_ref, kc_ref, vp_ref, vc_ref, o_ref):
    i = pl.program_id(1)
    q = q_ref[0]
    kcat = jnp.concatenate([kp_ref[0], kc_ref[0]], axis=0)
    vcat = jnp.concatenate([vp_ref[0], vc_ref[0]], axis=0)
    lane = lax.broadcasted_iota(jnp.int32, (BLOCK, LANES), 1)
    lo = lane < A_HEAD_DIM
    lane2 = lax.broadcasted_iota(jnp.int32, (2 * BLOCK, LANES), 1)
    lo2 = lane2 < A_HEAD_DIM
    row = lax.broadcasted_iota(jnp.int32, (2 * BLOCK, 2 * BLOCK), 0) % BLOCK
    col = lax.broadcasted_iota(jnp.int32, (2 * BLOCK, 2 * BLOCK), 1)
    dist = row + BLOCK - col
    mask = (dist >= 0) & (dist < BLOCK) & ((col >= BLOCK) | (i > 0))
    upper = lax.broadcasted_iota(jnp.int32, (2 * BLOCK, 1), 0) >= BLOCK
    group = A_HEADS // A_KV_HEADS
    zero = jnp.zeros((), q.dtype)
    for j in range(A_HEADS // 2):
        g = (2 * j) // group
        k2 = kcat[:, g * LANES:(g + 1) * LANES]
        v2 = vcat[:, g * LANES:(g + 1) * LANES]
        qp = q[:, j * LANES:(j + 1) * LANES]
        qs = jnp.concatenate([jnp.where(lo, qp, zero), jnp.where(lo, zero, qp)], axis=0)
        s = _dot_nt(qs, k2) * (A_HEAD_DIM ** -0.5)
        s = jnp.where(mask, s, NEG_BIG)
        sink = jnp.where(upper, sink_ref[2 * j + 1], sink_ref[2 * j])
        m = jnp.maximum(jnp.max(s, axis=1, keepdims=True), sink)
        p = jnp.exp(s - m)
        l = jnp.sum(p, axis=1, keepdims=True) + jnp.exp(sink - m)
        p = (p / l).astype(q.dtype)
        o = (_dot(p[:BLOCK], jnp.where(lo2, v2, zero))
             + _dot(p[BLOCK:], jnp.where(lo2, zero, v2)))
        o_ref[0, :, j * LANES:(j + 1) * LANES] = o.astype(o_ref.dtype)


def _swa(qk, vdup, sinks):
    nb, s, _ = qk.shape
    nblk = s // BLOCK
    aq = A_HEADS * A_HEAD_DIM
    kw = 2 * A_KV_HEADS * A_HEAD_DIM
    kcol = aq // kw
    prev = lambda b, i: (b, jnp.maximum(i - 1, 0), 0)
    return pl.pallas_call(
        _swa_kernel,
        out_shape=jax.ShapeDtypeStruct((nb, s, aq), BF16),
        grid=(nb, nblk),
        in_specs=[pl.BlockSpec(memory_space=pltpu.SMEM),
                  pl.BlockSpec((1, BLOCK, aq), lambda b, i: (b, i, 0)),
                  pl.BlockSpec((1, BLOCK, kw), lambda b, i: (b, jnp.maximum(i - 1, 0), kcol)),
                  pl.BlockSpec((1, BLOCK, kw), lambda b, i: (b, i, kcol)),
                  pl.BlockSpec((1, BLOCK, kw), prev),
                  pl.BlockSpec((1, BLOCK, kw), lambda b, i: (b, i, 0))],
        out_specs=pl.BlockSpec((1, BLOCK, aq), lambda b, i: (b, i, 0)),
        compiler_params=_cparams(("parallel", "parallel")),
        name="swa_sinks",
    )(sinks.astype(F32), qk, qk, qk, vdup, vdup)


RWKV_CHUNK = 64


def _bdot(a, b):
    return _dot(a.astype(BF16), b.astype(BF16))


def _bdot_nt(a, b):
    return _dot_nt(a.astype(BF16), b.astype(BF16))


def _bdot_tn(a, b):
    return _dot_tn(a.astype(BF16), b.astype(BF16))


def _sigmoid(x):
    return 1.0 / (1.0 + jnp.exp(-x))


def _rwkv_pre_kernel(z_ref, zp_ref, mu_ref, w0_ref, w2_ref, a0_ref, a2_ref, g2_ref, kk_ref, ka_ref,
                     rk_ref, hsum_ref, hbc_ref, tri_ref,
                     qt_ref, y0_ref, mt_ref, nt_ref, g_ref, bonus_ref):
    c = pl.program_id(1)
    bw = B_HEADS * B_HEAD_DIM
    hd = B_HEAD_DIM
    z = z_ref[0]
    ch = z.shape[0]
    prev = jnp.where(c == 0, 0.0, zp_ref[0][7:8, :])
    rowi = lax.broadcasted_iota(jnp.int32, z.shape, 0)
    zs = jnp.where(rowi == 0, prev, pltpu.roll(z, 1, 0))
    zm = z + (zs - z) * mu_ref[...]
    r = zm[:, 0:bw]
    k = zm[:, bw:2 * bw]
    v = zm[:, 2 * bw:3 * bw]
    wd = zm[:, 3 * bw:3 * bw + LORA_PAD]
    ad = zm[:, 3 * bw + LORA_PAD:3 * bw + 2 * LORA_PAD]
    gd = zm[:, 3 * bw + 2 * LORA_PAD:]

    xw = -(w0_ref[...] + _bdot(jnp.tanh(wd), w2_ref[...]))
    softplus = jnp.maximum(xw, 0.0) + jnp.log(1.0 + jnp.exp(-jnp.abs(xw)))
    lw = -jnp.exp(-softplus - 0.5)
    a = _sigmoid(a0_ref[...] + _bdot(ad, a2_ref[...]))
    g_ref[0] = _bdot(_sigmoid(gd), g2_ref[...])

    hsum = hsum_ref[...]
    hbc = hbc_ref[...]
    per_head = lambda t: _split_dot(_split_dot(t, hsum), hbc)
    kkr = k * kk_ref[...]
    inv = 1.0 / jnp.maximum(jnp.sqrt(_split_dot(kkr * kkr, hsum)), 1e-12)
    kk = kkr * _split_dot(inv, hbc)
    kp = k * (1.0 + (a - 1.0) * ka_ref[...])
    bonus_ref[0] = per_head(r * kp * rk_ref[...]) * v

    lw_hi = lw.astype(BF16)
    lw_lo = (lw - lw_hi.astype(F32)).astype(BF16)
    cum = _dot(tri_ref[...], lw_hi) + _dot(tri_ref[...], lw_lo)
    cum_end = cum[ch - 1:ch, :]
    e_neg = jnp.exp(-cum)
    e_end = jnp.exp(cum_end - cum)
    abar = -kk * jnp.exp(cum - lw)
    rbar = r * jnp.exp(cum)
    beta = kk * a
    bt = beta * e_neg
    kt = kp * e_neg
    bh = beta * e_end
    kh = kp * e_end
    w_end = jnp.exp(cum_end)

    ri = lax.broadcasted_iota(jnp.int32, (ch, ch), 0)
    ci = lax.broadcasted_iota(jnp.int32, (ch, ch), 1)
    strict = ri > ci
    incl = ri >= ci
    eye = (ri == ci).astype(F32)
    ek = (lax.broadcasted_iota(jnp.int32, (hd, hd), 0) == lax.broadcasted_iota(jnp.int32, (hd, hd), 1))
    for h in range(B_HEADS):
        sl = slice(h * hd, (h + 1) * hd)
        a_h, r_h, v_h = abar[:, sl], rbar[:, sl], v[:, sl]
        lab = jnp.where(strict, _bdot_nt(a_h, bt[:, sl]), 0.0)
        lak = jnp.where(strict, _bdot_nt(a_h, kt[:, sl]), 0.0)
        rb = jnp.where(incl, _bdot_nt(r_h, bt[:, sl]), 0.0)
        rk = jnp.where(incl, _bdot_nt(r_h, kt[:, sl]), 0.0)
        p = eye + lab
        lp = lab
        for _ in range(int(math.log2(ch)) - 1):
            lp = _bdot(lp, lp)
            p = p + _bdot(p, lp)
        x = jnp.concatenate([a_h, _bdot(lak, v_h)], axis=1)
        tx = _bdot(p, x)
        rtx = _bdot(rb, tx)
        qt_ref[0, :, sl] = r_h + rtx[:, :hd]
        y0_ref[0, :, sl] = rtx[:, hd:] + _bdot(rk, v_h)
        btx = _bdot_tn(bh[:, sl], tx)
        mt_ref[0, 0, h] = jnp.where(ek, w_end[:, sl], 0.0) + btx[:, :hd]
        nt_ref[0, 0, h] = btx[:, hd:] + _bdot_tn(kh[:, sl], v_h)


def _rwkv_scan_kernel(qt_ref, y0_ref, mt_ref, nt_ref, g_ref, bonus_ref, lng_ref, lnb_ref,
                      hsum_ref, hbc_ref, o_ref, st_scr, y_scr):
    hd = B_HEAD_DIM

    @pl.when(pl.program_id(1) == 0)
    def _():
        st_scr[...] = jnp.zeros_like(st_scr)

    for h in range(B_HEADS):
        sl = slice(h * hd, (h + 1) * hd)
        st = st_scr[h]
        y_scr[:, sl] = _bdot(qt_ref[0, :, sl], st) + y0_ref[0, :, sl]
        st_scr[h] = _bdot(mt_ref[0, 0, h], st) + nt_ref[0, 0, h]
    y = y_scr[...]
    hsum = hsum_ref[...]
    hbc = hbc_ref[...]
    per_head_mean = lambda t: _split_dot(_split_dot(t, hsum) * (1.0 / hd), hbc)
    d = y - per_head_mean(y)
    yn = d * lax.rsqrt(per_head_mean(d * d) + GN_EPS)
    o_ref[0] = (((yn * lng_ref[...] + lnb_ref[...]) + bonus_ref[0]) * g_ref[0]).astype(o_ref.dtype)


def _rwkv(zb, mu, w0, w2, a0, a2, g2, k_k, k_a, r_k, lnx_g, lnx_b):
    nb, s, zw = zb.shape
    bw = B_HEADS * B_HEAD_DIM
    ch = _tile(s, RWKV_CHUNK, unit=8)
    nc = s // ch
    head_of_lane = jnp.arange(bw) // B_HEAD_DIM
    hsum = (head_of_lane[:, None] == jnp.arange(LANES)[None, :]).astype(BF16)
    hbc = hsum.T
    tri = (jnp.arange(ch)[:, None] >= jnp.arange(ch)[None, :]).astype(BF16)
    row = lambda t: t.reshape(1, -1).astype(F32)
    const = lambda shape: pl.BlockSpec(shape, lambda b, c: (0,) * len(shape))
    per_chunk = pl.BlockSpec((1, ch, bw), lambda b, c: (b, c, 0))
    per_chunk_mat = pl.BlockSpec((1, 1, B_HEADS, B_HEAD_DIM, B_HEAD_DIM), lambda b, c: (b, c, 0, 0, 0))
    mat_shape = jax.ShapeDtypeStruct((nb, nc, B_HEADS, B_HEAD_DIM, B_HEAD_DIM), F32)
    vec_shape = jax.ShapeDtypeStruct((nb, s, bw), F32)
    qt, y0, mt, nt, g, bonus = pl.pallas_call(
        _rwkv_pre_kernel,
        out_shape=[vec_shape, vec_shape, mat_shape, mat_shape, vec_shape, vec_shape],
        grid=(nb, nc),
        in_specs=[pl.BlockSpec((1, ch, zw), lambda b, c: (b, c, 0)),
                  pl.BlockSpec((1, 8, zw), lambda b, c: (b, jnp.maximum(c * (ch // 8) - 1, 0), 0)),
                  const((1, zw)), const((1, bw)), const(w2.shape), const((1, bw)), const(a2.shape),
                  const(g2.shape), const((1, bw)), const((1, bw)), const((1, bw)),
                  const(hsum.shape), const(hbc.shape), const(tri.shape)],
        out_specs=[per_chunk, per_chunk, per_chunk_mat, per_chunk_mat, per_chunk, per_chunk],
        compiler_params=_cparams(("parallel", "parallel")),
        name="rwkv7_chunk_prepare",
    )(zb, zb, row(mu), row(w0), w2.astype(BF16), row(a0), a2.astype(BF16), g2.astype(BF16),
      row(k_k), row(k_a), row(r_k), hsum, hbc, tri)
    return pl.pallas_call(
        _rwkv_scan_kernel,
        out_shape=jax.ShapeDtypeStruct((nb, s, bw), BF16),
        grid=(nb, nc),
        in_specs=[per_chunk, per_chunk, per_chunk_mat, per_chunk_mat, per_chunk, per_chunk,
                  const((1, bw)), const((1, bw)), const(hsum.shape), const(hbc.shape)],
        out_specs=per_chunk,
        scratch_shapes=[pltpu.VMEM((B_HEADS, B_HEAD_DIM, B_HEAD_DIM), F32),
                        pltpu.VMEM((ch, bw), F32)],
        compiler_params=_cparams(("parallel", "arbitrary")),
        name="rwkv7_chunk_scan",
    )(qt, y0, mt, nt, g, bonus, row(lnx_g), row(lnx_b), hsum, hbc)


def _pad_rows(w, n):
    return jnp.concatenate([w, jnp.zeros((n - w.shape[0],) + w.shape[1:], w.dtype)], axis=0)


def _pad_cols(w, n):
    return jnp.concatenate([w, jnp.zeros(w.shape[:-1] + (n - w.shape[-1],), w.dtype)], axis=-1)


def _even_rwkv(x, scale, shift, w_in, mu, w0, w2, a0, a2, g2, k_k, k_a, r_k, lnx_g, lnx_b):
    bw = B_HEADS * B_HEAD_DIM
    a_in = (A_HEADS + 2 * A_KV_HEADS) * A_HEAD_DIM
    lora_d, lora_a = w2.shape[0], a2.shape[0]
    wb = w_in[:, a_in:]
    o = 3 * bw
    pack = lambda t: jnp.concatenate(
        [t[..., :o], _pad_cols(t[..., o:o + lora_d], LORA_PAD),
         _pad_cols(t[..., o + lora_d:o + lora_d + lora_a], LORA_PAD), t[..., o + lora_d + lora_a:]], axis=-1)
    wbp = pack(wb).astype(BF16)
    zb = _project(x, scale, shift, wbp, out_dtype=F32, tn=512)
    return _rwkv(zb, pack(mu), w0, _pad_rows(w2, LORA_PAD), a0, _pad_rows(a2, LORA_PAD), g2,
                 k_k, k_a, r_k.reshape(-1), lnx_g, lnx_b)


def _even_swa(x, scale, shift, positions, w_in, sinks):
    aq = A_HEADS * A_HEAD_DIM
    akv = A_KV_HEADS * A_HEAD_DIM
    wq = w_in[:, :aq]
    wk = w_in[:, aq:aq + akv]
    wv = w_in[:, aq + akv:aq + 2 * akv]
    dup = lambda w: jnp.concatenate(
        [w[:, h * A_HEAD_DIM:(h + 1) * A_HEAD_DIM] for h in range(A_KV_HEADS) for _ in range(2)], axis=1)
    cos, sin = _rot_tables(positions, A_HEAD_DIM // 2)
    w_qk = jnp.concatenate([wq, dup(wk)], axis=1).astype(BF16)
    qk = _project(x, scale, shift, w_qk, out_dtype=BF16, tn=w_qk.shape[1],
                  rot=(A_HEAD_DIM // 2, cos, sin))
    vdup = _project(x, scale, shift, dup(wv).astype(BF16), out_dtype=BF16, tn=2 * akv)
    return _swa(qk, vdup, sinks)


DSA_TQ = 128
INT_MIN = -2 ** 31


def _sortable(v):
    bits = pltpu.bitcast(v, jnp.int32)
    return jnp.where(bits < 0, bits ^ jnp.int32(0x7FFFFFFF), bits)


def _dsa_index_kernel(qi_ref, ki_ref, wi_ref, o_ref, key_scr, qs_scr, wb_scr, *, n_sel, tk):
    i = pl.program_id(1)
    tq = qi_ref.shape[1]
    s_len = ki_ref.shape[1]
    n_kt = ((i + 1) * tq + tk - 1) // tk
    lane = lax.broadcasted_iota(jnp.int32, (tq, LANES), 1)
    lo = lane < IDX_DIM
    qi = qi_ref[0]
    zero = jnp.zeros((), qi.dtype)
    w = wi_ref[0] * (IDX_HEADS ** -0.5 * IDX_DIM ** -0.5)
    for j in range(IDX_HEADS // 2):
        qp = qi[:, j * LANES:(j + 1) * LANES]
        qs_scr[(2 * j) * tq:(2 * j + 1) * tq, :] = jnp.where(lo, qp, zero)
        qs_scr[(2 * j + 1) * tq:(2 * j + 2) * tq, :] = jnp.where(lo, zero, qp)
    for h in range(IDX_HEADS):
        wb_scr[h] = jnp.broadcast_to(w[:, h:h + 1], (tq, LANES))
    row = i * tq + lax.broadcasted_iota(jnp.int32, (tq, tk), 0)
    coli = lax.broadcasted_iota(jnp.int32, (tq, tk), 1)

    def score_tile(kt, carry):
        start = pl.multiple_of(kt * tk, tk)
        sc = _dot_nt(qs_scr[...], ki_ref[0, pl.ds(start, tk), :])
        acc = jnp.zeros((tq, tk), F32)
        for h in range(IDX_HEADS):
            wt = jnp.concatenate([wb_scr[h]] * (tk // LANES), axis=1)
            acc = acc + jnp.maximum(sc[h * tq:(h + 1) * tq], 0.0) * wt
        acc = jnp.where(start + coli <= row, acc, -jnp.inf)
        key_scr[:, pl.ds(start, tk)] = _sortable(acc)
        return carry

    lax.fori_loop(0, n_kt, score_tile, 0)

    def count_ge(cand):
        def body(kt, cnt):
            start = pl.multiple_of(kt * tk, tk)
            ge = (key_scr[:, pl.ds(start, tk)] >= cand).astype(jnp.int32)
            for t in range(tk // LANES):
                cnt = cnt + ge[:, t * LANES:(t + 1) * LANES]
            return cnt
        cnt = lax.fori_loop(0, n_kt, body, jnp.zeros((tq, LANES), jnp.int32))
        return jnp.sum(cnt, axis=1, keepdims=True)

    thr = jnp.where(count_ge(jnp.zeros((tq, 1), jnp.int32)) >= n_sel, 0, INT_MIN).astype(jnp.int32)

    def bit_step(it, thr):
        cand = thr | (jnp.int32(1) << (30 - it))
        return jnp.where(count_ge(cand) >= n_sel, cand, thr)

    thr = lax.fori_loop(0, 31, bit_step, thr)

    def emit(kt, carry):
        start = pl.multiple_of(kt * tk, tk)
        sel = (key_scr[:, pl.ds(start, tk)] >= thr) & (start + coli <= row)
        o_ref[0, :, pl.ds(start, tk)] = jnp.where(sel, 0.0, NEG_BIG).astype(o_ref.dtype)
        return carry

    lax.fori_loop(0, n_kt, emit, 0)

    def fill(kt, carry):
        start = pl.multiple_of(kt * tk, tk)
        o_ref[0, :, pl.ds(start, tk)] = jnp.full((tq, tk), NEG_BIG, o_ref.dtype)
        return carry

    lax.fori_loop(n_kt, s_len // tk, fill, 0)


def _dsa_attn_kernel(q_ref, k_ref, v_ref, b_ref, o_ref, *, tk):
    i = pl.program_id(2)
    tq = q_ref.shape[1]
    group = C_HEADS // C_KV_HEADS
    n_kt = ((i + 1) * tq + tk - 1) // tk
    q = q_ref[0]
    qs = jnp.concatenate([q[:, g * LANES:(g + 1) * LANES] for g in range(group)], axis=0)

    def body(kt, carry):
        m, l, acc = carry
        start = pl.multiple_of(kt * tk, tk)
        s = _dot_nt(qs, k_ref[0, pl.ds(start, tk), :]) * (C_HEAD_DIM ** -0.5)
        bias = b_ref[0, :, pl.ds(start, tk)].astype(F32)
        s = s + jnp.concatenate([bias] * group, axis=0)
        m_new = jnp.maximum(m, jnp.max(s, axis=1, keepdims=True))
        alpha = jnp.exp(m - m_new)
        p = jnp.exp(s - m_new)
        l = alpha * l + jnp.sum(p, axis=1, keepdims=True)
        acc = alpha * acc + _dot(p.astype(BF16), v_ref[0, pl.ds(start, tk), :])
        return m_new, l, acc

    init = (jnp.full((group * tq, 1), NEG_BIG, F32), jnp.zeros((group * tq, 1), F32),
            jnp.zeros((group * tq, C_HEAD_DIM), F32))
    _, l, acc = lax.fori_loop(0, n_kt, body, init)
    out = acc / l
    for g in range(group):
        o_ref[0, :, g * LANES:(g + 1) * LANES] = out[g * tq:(g + 1) * tq].astype(o_ref.dtype)


def _dsa(qk, qik, vw):
    nb, s, _ = qk.shape
    n_sel = min(TOPK_MAX, s // 4)
    tq = _tile(s, DSA_TQ)
    cq = C_HEADS * C_HEAD_DIM
    ckv = C_KV_HEADS * C_HEAD_DIM
    iq = IDX_HEADS * IDX_DIM
    tk_i = _tile(s, 256)
    bias = pl.pallas_call(
        functools.partial(_dsa_index_kernel, n_sel=n_sel, tk=tk_i),
        out_shape=jax.ShapeDtypeStruct((nb, s, s), BF16),
        grid=(nb, s // tq),
        in_specs=[pl.BlockSpec((1, tq, iq), lambda b, i: (b, i, 0)),
                  pl.BlockSpec((1, s, LANES), lambda b, i: (b, 0, iq // LANES)),
                  pl.BlockSpec((1, tq, LANES), lambda b, i: (b, i, ckv // LANES))],
        out_specs=pl.BlockSpec((1, tq, s), lambda b, i: (b, i, 0)),
        scratch_shapes=[pltpu.VMEM((tq, s), jnp.int32),
                        pltpu.VMEM((IDX_HEADS * tq, LANES), BF16),
                        pltpu.VMEM((IDX_HEADS, tq, LANES), F32)],
        compiler_params=_cparams(("parallel", "parallel")),
        name="dsa_indexer_topk",
    )(qik, qik, vw)
    v = vw[..., :ckv].astype(BF16)
    tk_a = _tile(s, 512)
    gw = cq // C_KV_HEADS
    return pl.pallas_call(
        functools.partial(_dsa_attn_kernel, tk=tk_a),
        out_shape=jax.ShapeDtypeStruct((nb, s, cq), BF16),
        grid=(nb, C_KV_HEADS, s // tq),
        in_specs=[pl.BlockSpec((1, tq, gw), lambda b, h, i: (b, i, h)),
                  pl.BlockSpec((1, s, C_HEAD_DIM), lambda b, h, i: (b, 0, cq // C_HEAD_DIM + h)),
                  pl.BlockSpec((1, s, C_HEAD_DIM), lambda b, h, i: (b, 0, h)),
                  pl.BlockSpec((1, tq, s), lambda b, h, i: (b, i, 0))],
        out_specs=pl.BlockSpec((1, tq, gw), lambda b, h, i: (b, i, h)),
        compiler_params=_cparams(("parallel", "parallel", "parallel")),
        name="dsa_masked_attention",
    )(qk, qk, v, bias)


def _odd_mixer(x, scale, shift, positions, w_in):
    cq = C_HEADS * C_HEAD_DIM
    ckv = C_KV_HEADS * C_HEAD_DIM
    iq = IDX_HEADS * IDX_DIM
    o1, o2, o3, o4 = cq + ckv, cq + 2 * ckv, cq + 2 * ckv + iq, cq + 2 * ckv + iq + IDX_DIM
    w_qk = w_in[:, :o1].astype(BF16)
    w_v = w_in[:, o1:o2]
    w_ki = w_in[:, o3:o4]
    w_qik = jnp.concatenate([w_in[:, o2:o3], w_ki, w_ki], axis=1).astype(BF16)
    w_vw = jnp.concatenate([w_v, _pad_cols(w_in[:, o4:], LANES)], axis=1).astype(BF16)
    cos64, sin64 = _rot_tables(positions, C_HEAD_DIM // 2)
    cos32, sin32 = _rot_tables(positions, IDX_DIM // 2)
    qk = _project(x, scale, shift, w_qk, out_dtype=BF16, tn=1280, rot=(C_HEAD_DIM // 2, cos64, sin64))
    qik = _project(x, scale, shift, w_qik, out_dtype=BF16, tn=w_qik.shape[1], rot=(IDX_DIM // 2, cos32, sin32))
    vw = _project(x, scale, shift, w_vw, out_dtype=F32, tn=w_vw.shape[1])
    return _dsa(qk, qik, vw)


PEER_ROUTE_TQ = 256
PEER_TQ = 512
PEER_EC = 512


def _top_values(s, n):
    out = []
    for _ in range(n):
        m = jnp.max(s, axis=0, keepdims=True)
        out.append(m)
        s = jnp.where(s == m, -jnp.inf, s)
    return out


def _peer_route_kernel(x_ref, sc_ref, sh_ref, wq_ref, sk_ref, s0_ref, s1_ref, aux_ref):
    h = (x_ref[0] * (1.0 + sc_ref[0]) + sh_ref[0]).astype(BF16)
    q = _dot(h, wq_ref[...]).astype(BF16)
    for hd in range(PEER_HEADS):
        st = []
        for p in range(2):
            col = (2 * hd + p) * LANES
            st.append(_dot_nt(sk_ref[2 * hd + p], q[:, col:col + LANES]))
        s0_ref[0, hd] = st[0]
        s1_ref[0, hd] = st[1]
        v0 = _top_values(st[0], PEER_TOPK)
        v1 = _top_values(st[1], PEER_TOPK)
        v1all = jnp.concatenate(v1, axis=0)
        cand = jnp.concatenate([v0[a] + v1all for a in range(PEER_TOPK)], axis=0)
        tau = _top_values(cand, PEER_TOPK)[-1]
        best = v0[0] + v1[0]
        z = jnp.sum(jnp.where(cand >= tau, jnp.exp(cand - best), 0.0), axis=0, keepdims=True)
        aux_ref[0, 0, hd:hd + 1, :] = tau
        aux_ref[0, 1, hd:hd + 1, :] = v0[0]
        aux_ref[0, 2, hd:hd + 1, :] = v1[0] + jnp.log(z)


def _gelu(a):
    return 0.5 * a * (1.0 + lax.erf(a * (2.0 ** -0.5)))


def _peer_main_kernel(x_ref, sc_ref, sh_ref, gate_ref, u_ref, vt_ref, s0_ref, s1_ref, aux_ref,
                      g_ref, b_ref, o_ref, ht_scr, acc_scr, e1_scr):
    k = pl.program_id(2)
    ec = u_ref.shape[0]
    rows_per_chunk = ec // PEER_NKEYS

    @pl.when(k == 0)
    def _():
        h = x_ref[0] * (1.0 + sc_ref[0]) + sh_ref[0]
        ht_scr[...] = h.T.astype(BF16)
        acc_scr[...] = jnp.zeros_like(acc_scr)
        for hd in range(PEER_HEADS):
            e1_scr[hd] = jnp.exp(s1_ref[0, hd] - aux_ref[0, 2, hd:hd + 1, :])

    a = _dot(u_ref[...], ht_scr[...])
    parts = []
    for r in range(rows_per_chunk):
        n1 = k * rows_per_chunk + r
        gate = jnp.zeros((PEER_NKEYS, a.shape[1]), F32)
        for hd in range(PEER_HEADS):
            s0row = s0_ref[0, hd, pl.ds(n1, 1), :]
            sel = (s0row + s1_ref[0, hd]) >= aux_ref[0, 0, hd:hd + 1, :]
            e0row = jnp.exp(s0row - aux_ref[0, 1, hd:hd + 1, :])
            gate = gate + jnp.where(sel, e1_scr[hd], 0.0) * e0row
        parts.append((_gelu(a[r * PEER_NKEYS:(r + 1) * PEER_NKEYS]) * gate).astype(BF16))
    acc_scr[...] += _dot(vt_ref[...], jnp.concatenate(parts, axis=0))

    @pl.when(k == pl.num_programs(2) - 1)
    def _():
        v = ALPHA * x_ref[0] + (1.0 + gate_ref[0]) * acc_scr[...].T
        o_ref[0] = _layer_norm(v, g_ref[...], b_ref[...])


def _peer_layer(x, scale, shift, gate, wq, subkeys, u_tab, v_tab, ln_g, ln_b):
    nb, s, d = x.shape
    ne = u_tab.shape[0]
    qw = wq.shape[1]
    sk = jnp.transpose(subkeys, (1, 0, 2, 3)).reshape(2 * PEER_HEADS, PEER_NKEYS, -1).astype(BF16)
    tq = _tile(s, PEER_ROUTE_TQ)
    sshape = jax.ShapeDtypeStruct((nb, PEER_HEADS, PEER_NKEYS, s), F32)
    sspec = lambda t: pl.BlockSpec((1, PEER_HEADS, PEER_NKEYS, t), lambda b, i, *_: (b, 0, 0, i))
    aspec = lambda t: pl.BlockSpec((1, 3, PEER_HEADS, t), lambda b, i, *_: (b, 0, 0, i))
    mod_spec = pl.BlockSpec((1, 1, d), lambda b, i, *_: (b, 0, 0))
    s0, s1, aux = pl.pallas_call(
        _peer_route_kernel,
        out_shape=[sshape, sshape, jax.ShapeDtypeStruct((nb, 3, PEER_HEADS, s), F32)],
        grid=(nb, s // tq),
        in_specs=[pl.BlockSpec((1, tq, d), lambda b, i: (b, i, 0)), mod_spec, mod_spec,
                  pl.BlockSpec((d, qw), lambda b, i: (0, 0)),
                  pl.BlockSpec(sk.shape, lambda b, i: (0, 0, 0))],
        out_specs=[sspec(tq), sspec(tq), aspec(tq)],
        compiler_params=_cparams(("parallel", "parallel")),
        name="peer_router",
    )(x, scale, shift, wq.astype(BF16), sk)
    tq = _tile(s, PEER_TQ)
    ec = _tile(ne, PEER_EC)
    return pl.pallas_call(
        _peer_main_kernel,
        out_shape=jax.ShapeDtypeStruct((nb, s, d), F32),
        grid=(nb, s // tq, ne // ec),
        in_specs=[pl.BlockSpec((1, tq, d), lambda b, i, k: (b, i, 0)), mod_spec, mod_spec, mod_spec,
                  pl.BlockSpec((ec, d), lambda b, i, k: (k, 0)),
                  pl.BlockSpec((d, ec), lambda b, i, k: (0, k)),
                  sspec(tq), sspec(tq), aspec(tq),
                  pl.BlockSpec((1, d), lambda b, i, k: (0, 0)),
                  pl.BlockSpec((1, d), lambda b, i, k: (0, 0))],
        out_specs=pl.BlockSpec((1, tq, d), lambda b, i, k: (b, i, 0)),
        scratch_shapes=[pltpu.VMEM((d, tq), BF16), pltpu.VMEM((d, tq), F32),
                        pltpu.VMEM((PEER_HEADS, PEER_NKEYS, tq), F32)],
        compiler_params=_cparams(("parallel", "parallel", "arbitrary")),
        name="peer_experts",
    )(x, scale, shift, gate, u_tab.astype(BF16), v_tab.T.astype(BF16), s0, s1, aux,
      ln_g.reshape(1, d), ln_b.reshape(1, d))


def _split_mod(m):
    shift, scale, gate = jnp.split(m[:, None, :], 3, axis=-1)
    return shift, scale, gate


def kernel(x, c, positions, ev_w_in, ev_w_out, ev_sinks, ev_mu, ev_w0, ev_w2, ev_a0, ev_a2, ev_g2,
           ev_k_k, ev_k_a, ev_r_k, ev_lnx_g, ev_lnx_b, od_w_in, od_w_out, mod_w, mod_b, ln_g, ln_b,
           peer_wq, peer_subkeys, peer_u, peer_v):
    mods = _modulation(c, mod_w, mod_b)
    aq = A_HEADS * A_HEAD_DIM
    for layer in range(DEPTH):
        j = layer // 2
        shift, scale, gate = _split_mod(mods[2 * layer])
        if layer % 2 == 0:
            ya = _even_swa(x, scale, shift, positions, ev_w_in[j], ev_sinks[j])
            yb = _even_rwkv(x, scale, shift, ev_w_in[j], ev_mu[j], ev_w0[j], ev_w2[j], ev_a0[j],
                            ev_a2[j], ev_g2[j], ev_k_k[j], ev_k_a[j], ev_r_k[j], ev_lnx_g[j], ev_lnx_b[j])
            w_out = ev_w_out[j].astype(BF16)
            x = _outproj_ln([ya, yb], [w_out[:aq], w_out[aq:]], x, gate, ln_g[layer, 0], ln_b[layer, 0])
        else:
            y = _odd_mixer(x, scale, shift, positions, od_w_in[j])
            x = _outproj_ln([y], [od_w_out[j].astype(BF16)], x, gate, ln_g[layer, 0], ln_b[layer, 0])
        shift, scale, gate = _split_mod(mods[2 * layer + 1])
        x = _peer_layer(x, scale, shift, gate, peer_wq[layer], peer_subkeys[layer], peer_u[layer],
                        peer_v[layer], ln_g[layer, 1], ln_b[layer, 1])
    return x
```

```python
import functools
import math

import jax
import jax.numpy as jnp
from jax import lax
from jax.experimental import pallas as pl
from jax.experimental.pallas import tpu as pltpu

F32 = jnp.float32
BF16 = jnp.bfloat16

LANES = 128
DEPTH = 2
ALPHA = (2 * DEPTH) ** 0.25
LN_EPS = 1e-5
ROPE_THETA = 10000.0
BLOCK = 128
A_HEADS, A_KV_HEADS, A_HEAD_DIM = 16, 2, 64
B_HEADS, B_HEAD_DIM = 16, 64
GN_EPS = 64e-5
LORA_PAD = 128
C_HEADS, C_KV_HEADS, C_HEAD_DIM = 16, 4, 128
IDX_HEADS, IDX_DIM = 16, 64
TOPK_MAX = 256
PEER_HEADS, PEER_NKEYS, PEER_TOPK = 8, 128, 16
NEG_BIG = -1e30
VMEM_LIMIT = 56 * 1024 * 1024


def _cparams(sem):
    return pltpu.CompilerParams(dimension_semantics=sem, vmem_limit_bytes=VMEM_LIMIT)


def _tile(n, pref, unit=LANES):
    if n <= pref:
        return n
    t = (pref // unit) * unit
    while n % t:
        t -= unit
    assert t > 0, (n, pref)
    return t


def _dot(a, b):
    return jnp.dot(a, b, preferred_element_type=F32)


def _dot_nt(a, b):
    return lax.dot_general(a, b, (((1,), (1,)), ((), ())), preferred_element_type=F32)


def _dot_tn(a, b):
    return lax.dot_general(a, b, (((0,), (0,)), ((), ())), preferred_element_type=F32)


def _split_dot(a, b_bf16):
    hi = a.astype(BF16)
    lo = (a - hi.astype(F32)).astype(BF16)
    return _dot(hi, b_bf16) + _dot(lo, b_bf16)


def _layer_norm(v, g, b):
    mu = jnp.mean(v, axis=-1, keepdims=True)
    d = v - mu
    var = jnp.mean(d * d, axis=-1, keepdims=True)
    return d * lax.rsqrt(var + LN_EPS) * g + b


def _mod_kernel(c_ref, w_ref, b_ref, o_ref):
    c = c_ref[...]
    ca = c * (1.0 / (1.0 + jnp.exp(-c)))
    o_ref[0] = _split_dot(ca, w_ref[0].astype(BF16)) + _split_dot(
        ca, (w_ref[0] - w_ref[0].astype(BF16).astype(F32)).astype(BF16)) + b_ref[0]


def _modulation(c, mod_w, mod_b):
    nb, d = c.shape
    nl = mod_w.shape[0] * mod_w.shape[1]
    n3 = mod_w.shape[-1]
    rows = 8
    cp = jnp.zeros((rows, d), F32).at[:nb].set(c)
    w = mod_w.reshape(nl, d, n3)
    b = mod_b.reshape(nl, 1, n3)
    tn = _tile(n3, 512)
    out = pl.pallas_call(
        _mod_kernel,
        out_shape=jax.ShapeDtypeStruct((nl, rows, n3), F32),
        grid=(nl, n3 // tn),
        in_specs=[pl.BlockSpec((rows, d), lambda l, j: (0, 0)),
                  pl.BlockSpec((1, d, tn), lambda l, j: (l, 0, j)),
                  pl.BlockSpec((1, 1, tn), lambda l, j: (l, 0, j))],
        out_specs=pl.BlockSpec((1, rows, tn), lambda l, j: (l, 0, j)),
        compiler_params=_cparams(("parallel", "parallel")),
        name="modulation",
    )(cp, w, b)
    return out[:, :nb]


def _proj_kernel(*refs, rot_half):
    if rot_half:
        x_ref, sc_ref, sh_ref, w_ref, cos_ref, sin_ref, o_ref, h_scr = refs
    else:
        x_ref, sc_ref, sh_ref, w_ref, o_ref, h_scr = refs

    @pl.when(pl.program_id(2) == 0)
    def _():
        h_scr[...] = (x_ref[0] * (1.0 + sc_ref[0]) + sh_ref[0]).astype(BF16)

    z = _dot(h_scr[...], w_ref[...])
    if rot_half:
        tn = z.shape[1]
        lane = lax.broadcasted_iota(jnp.int32, z.shape, 1)
        first = (lane % (2 * rot_half)) < rot_half
        swapped = jnp.where(first, pltpu.roll(z, tn - rot_half, 1), pltpu.roll(z, rot_half, 1))
        reps = tn // LANES
        cos = jnp.concatenate([cos_ref[0]] * reps, axis=1)
        sin = jnp.concatenate([sin_ref[0]] * reps, axis=1)
        z = z * cos + swapped * sin
    o_ref[0] = z.astype(o_ref.dtype)


def _project(x, scale, shift, w, *, out_dtype, tn, rot=None, tm=512):
    nb, s, d = x.shape
    n = w.shape[1]
    tm = _tile(s, tm)
    tn = _tile(n, tn)
    in_specs = [pl.BlockSpec((1, tm, d), lambda b, i, j: (b, i, 0)),
                pl.BlockSpec((1, 1, d), lambda b, i, j: (b, 0, 0)),
                pl.BlockSpec((1, 1, d), lambda b, i, j: (b, 0, 0)),
                pl.BlockSpec((d, tn), lambda b, i, j: (0, j))]
    args = [x, scale, shift, w]
    rot_half = 0
    if rot is not None:
        rot_half, cos, sin = rot
        in_specs += [pl.BlockSpec((1, tm, LANES), lambda b, i, j: (b, i, 0))] * 2
        args += [cos, sin]
    return pl.pallas_call(
        functools.partial(_proj_kernel, rot_half=rot_half),
        out_shape=jax.ShapeDtypeStruct((nb, s, n), out_dtype),
        grid=(nb, s // tm, n // tn),
        in_specs=in_specs,
        out_specs=pl.BlockSpec((1, tm, tn), lambda b, i, j: (b, i, j)),
        scratch_shapes=[pltpu.VMEM((tm, d), BF16)],
        compiler_params=_cparams(("parallel", "parallel", "arbitrary")),
        name="in_proj_rot%d" % rot_half,
    )(*args)


def _rot_tables(positions, rot_half):
    inv = ROPE_THETA ** (-jnp.arange(rot_half, dtype=F32) * (2.0 / (2 * rot_half)))
    ang = positions.astype(F32)[..., None] * inv
    cos, sin = jnp.cos(ang), jnp.sin(ang)
    reps = LANES // (2 * rot_half)
    cos_t = jnp.concatenate([cos, cos] * reps, axis=-1)
    sin_t = jnp.concatenate([-sin, sin] * reps, axis=-1)
    return cos_t, sin_t


def _outproj_kernel(*refs, n_in):
    ys = refs[:n_in]
    ws = refs[n_in:2 * n_in]
    x_ref, gate_ref, g_ref, b_ref, o_ref = refs[2 * n_in:]
    y = _dot(ys[0][0], ws[0][...])
    for y_ref, w_ref in zip(ys[1:], ws[1:]):
        y = y + _dot(y_ref[0], w_ref[...])
    v = ALPHA * x_ref[0] + (1.0 + gate_ref[0]) * y
    o_ref[0] = _layer_norm(v, g_ref[...], b_ref[...])


def _outproj_ln(ys, ws, x, gate, ln_g, ln_b, *, tm=256):
    nb, s, d = x.shape
    tm = _tile(s, tm)
    n_in = len(ys)
    in_specs = ([pl.BlockSpec((1, tm, y.shape[2]), lambda b, i: (b, i, 0)) for y in ys]
                + [pl.BlockSpec(w.shape, lambda b, i: (0, 0)) for w in ws]
                + [pl.BlockSpec((1, tm, d), lambda b, i: (b, i, 0)),
                   pl.BlockSpec((1, 1, d), lambda b, i: (b, 0, 0)),
                   pl.BlockSpec((1, d), lambda b, i: (0, 0)),
                   pl.BlockSpec((1, d), lambda b, i: (0, 0))])
    return pl.pallas_call(
        functools.partial(_outproj_kernel, n_in=n_in),
        out_shape=jax.ShapeDtypeStruct((nb, s, d), F32),
        grid=(nb, s // tm),
        in_specs=in_specs,
        out_specs=pl.BlockSpec((1, tm, d), lambda b, i: (b, i, 0)),
        compiler_params=_cparams(("parallel", "parallel")),
        name="out_proj_ln",
    )(*ys, *ws, x, gate, ln_g.reshape(1, d), ln_b.reshape(1, d))


def _swa_kernel(sink_ref, q_ref, kp_ref, kc_ref, vp_ref, vc_ref, o_ref):
    i = pl.program_id(1)
    q = q_ref[0]
    kcat = jnp.concatenate([kp_ref[0], kc_ref[0]], axis=0)
    vcat = jnp.concatenate([vp_ref[0], vc_ref[0]], axis=0)
    lane = lax.broadcasted_iota(jnp.int32, (BLOCK, LANES), 1)
    lo = lane < A_HEAD_DIM
    lane2 = lax.broadcasted_iota(jnp.int32, (2 * BLOCK, LANES), 1)
    lo2 = lane2 < A_HEAD_DIM
    row = lax.broadcasted_iota(jnp.int32, (2 * BLOCK, 2 * BLOCK), 0) % BLOCK
    col = lax.broadcasted_iota(jnp.int32, (2 * BLOCK, 2 * BLOCK), 1)
    dist = row + BLOCK - col
    mask = (dist >= 0) & (dist < BLOCK) & ((col >= BLOCK) | (i > 0))
    upper = lax.broadcasted_iota(jnp.int32, (2 * BLOCK, 1), 0) >= BLOCK
    group = A_HEADS // A_KV_HEADS
    zero = jnp.zeros((), q.dtype)
    for j in range(A_HEADS // 2):
        g = (2 * j) // group
        k2 = kcat[:, g * LANES:(g + 1) * LANES]
        v2 = vcat[:, g * LANES:(g + 1) * LANES]
        qp = q[:, j * LANES:(j + 1) * LANES]
        qs = jnp.concatenate([jnp.where(lo, qp, zero), jnp.where(lo, zero, qp)], axis=0)
        s = _dot_nt(qs, k2) * (A_HEAD_DIM ** -0.5)
        s = jnp.where(mask, s, NEG_BIG)
        sink = jnp.where(upper, sink_ref[2 * j + 1], sink_ref[2 * j])
        m = jnp.maximum(jnp.max(s, axis=1, keepdims=True), sink)
        p = jnp.exp(s - m)
        l = jnp.sum(p, axis=1, keepdims=True) + jnp.exp(sink - m)
        p = (p / l).astype(q.dtype)
        o = (_dot(p[:BLOCK], jnp.where(lo2, v2, zero))
             + _dot(p[BLOCK:], jnp.where(lo2, zero, v2)))
        o_ref[0, :, j * LANES:(j + 1) * LANES] = o.astype(o_ref.dtype)


def _swa(qk, vdup, sinks):
    nb, s, _ = qk.shape
    nblk = s // BLOCK
    aq = A_HEADS * A_HEAD_DIM
    kw = 2 * A_KV_HEADS * A_HEAD_DIM
    kcol = aq // kw
    prev = lambda b, i: (b, jnp.maximum(i - 1, 0), 0)
    return pl.pallas_call(
        _swa_kernel,
        out_shape=jax.ShapeDtypeStruct((nb, s, aq), BF16),
        grid=(nb, nblk),
        in_specs=[pl.BlockSpec(memory_space=pltpu.SMEM),
                  pl.BlockSpec((1, BLOCK, aq), lambda b, i: (b, i, 0)),
                  pl.BlockSpec((1, BLOCK, kw), lambda b, i: (b, jnp.maximum(i - 1, 0), kcol)),
                  pl.BlockSpec((1, BLOCK, kw), lambda b, i: (b, i, kcol)),
                  pl.BlockSpec((1, BLOCK, kw), prev),
                  pl.BlockSpec((1, BLOCK, kw), lambda b, i: (b, i, 0))],
        out_specs=pl.BlockSpec((1, BLOCK, aq), lambda b, i: (b, i, 0)),
        compiler_params=_cparams(("parallel", "parallel")),
        name="swa_sinks",
    )(sinks.astype(F32), qk, qk, qk, vdup, vdup)


RWKV_CHUNK = 64


def _bdot(a, b):
    return _dot(a.astype(BF16), b.astype(BF16))


def _bdot_nt(a, b):
    return _dot_nt(a.astype(BF16), b.astype(BF16))


def _bdot_tn(a, b):
    return _dot_tn(a.astype(BF16), b.astype(BF16))


def _sigmoid(x):
    return 1.0 / (1.0 + jnp.exp(-x))


def _rwkv_pre_kernel(z_ref, zp_ref, mu_ref, w0_ref, w2_ref, a0_ref, a2_ref, g2_ref, kk_ref, ka_ref,
                     rk_ref, hsum_ref, hbc_ref, tri_ref,
                     qt_ref, y0_ref, mt_ref, nt_ref, g_ref, bonus_ref):
    c = pl.program_id(1)
    bw = B_HEADS * B_HEAD_DIM
    hd = B_HEAD_DIM
    z = z_ref[0]
    ch = z.shape[0]
    prev = jnp.where(c == 0, 0.0, zp_ref[0][7:8, :])
    rowi = lax.broadcasted_iota(jnp.int32, z.shape, 0)
    zs = jnp.where(rowi == 0, prev, pltpu.roll(z, 1, 0))
    zm = z + (zs - z) * mu_ref[...]
    r = zm[:, 0:bw]
    k = zm[:, bw:2 * bw]
    v = zm[:, 2 * bw:3 * bw]
    wd = zm[:, 3 * bw:3 * bw + LORA_PAD]
    ad = zm[:, 3 * bw + LORA_PAD:3 * bw + 2 * LORA_PAD]
    gd = zm[:, 3 * bw + 2 * LORA_PAD:]

    xw = -(w0_ref[...] + _bdot(jnp.tanh(wd), w2_ref[...]))
    softplus = jnp.maximum(xw, 0.0) + jnp.log(1.0 + jnp.exp(-jnp.abs(xw)))
    lw = -jnp.exp(-softplus - 0.5)
    a = _sigmoid(a0_ref[...] + _bdot(ad, a2_ref[...]))
    g_ref[0] = _bdot(_sigmoid(gd), g2_ref[...])

    hsum = hsum_ref[...]
    hbc = hbc_ref[...]
    per_head = lambda t: _split_dot(_split_dot(t, hsum), hbc)
    kkr = k * kk_ref[...]
    inv = 1.0 / jnp.maximum(jnp.sqrt(_split_dot(kkr * kkr, hsum)), 1e-12)
    kk = kkr * _split_dot(inv, hbc)
    kp = k * (1.0 + (a - 1.0) * ka_ref[...])
    bonus_ref[0] = per_head(r * kp * rk_ref[...]) * v

    lw_hi = lw.astype(BF16)
    lw_lo = (lw - lw_hi.astype(F32)).astype(BF16)
    cum = _dot(tri_ref[...], lw_hi) + _dot(tri_ref[...], lw_lo)
    cum_end = cum[ch - 1:ch, :]
    e_neg = jnp.exp(-cum)
    e_end = jnp.exp(cum_end - cum)
    abar = -kk * jnp.exp(cum - lw)
    rbar = r * jnp.exp(cum)
    beta = kk * a
    bt = beta * e_neg
    kt = kp * e_neg
    bh = beta * e_end
    kh = kp * e_end
    w_end = jnp.exp(cum_end)

    ri = lax.broadcasted_iota(jnp.int32, (ch, ch), 0)
    ci = lax.broadcasted_iota(jnp.int32, (ch, ch), 1)
    strict = ri > ci
    incl = ri >= ci
    eye = (ri == ci).astype(F32)
    ek = (lax.broadcasted_iota(jnp.int32, (hd, hd), 0) == lax.broadcasted_iota(jnp.int32, (hd, hd), 1))
    heads = range(B_HEADS)
    sls = [slice(h * hd, (h + 1) * hd) for h in heads]
    lab = [jnp.where(strict, _bdot_nt(abar[:, sl], bt[:, sl]), 0.0) for sl in sls]
    lak = [jnp.where(strict, _bdot_nt(abar[:, sl], kt[:, sl]), 0.0) for sl in sls]
    rb = [jnp.where(incl, _bdot_nt(rbar[:, sl], bt[:, sl]), 0.0) for sl in sls]
    rk = [jnp.where(incl, _bdot_nt(rbar[:, sl], kt[:, sl]), 0.0) for sl in sls]
    p = [eye + l for l in lab]
    lp = lab
    for _ in range(int(math.log2(ch)) - 1):
        lp = [_bdot(l, l) for l in lp]
        p = [ph + _bdot(ph, l) for ph, l in zip(p, lp)]
    x = [jnp.concatenate([abar[:, sl], _bdot(lk, v[:, sl])], axis=1) for sl, lk in zip(sls, lak)]
    tx = [_bdot(ph, xh) for ph, xh in zip(p, x)]
    rtx = [_bdot(rbh, th) for rbh, th in zip(rb, tx)]
    btx = [_bdot_tn(bh[:, sl], th) for sl, th in zip(sls, tx)]
    for h, sl in zip(heads, sls):
        qt_ref[0, :, sl] = rbar[:, sl] + rtx[h][:, :hd]
        y0_ref[0, :, sl] = rtx[h][:, hd:] + _bdot(rk[h], v[:, sl])
        mt_ref[0, 0, h] = jnp.where(ek, w_end[:, sl], 0.0) + btx[h][:, :hd]
        nt_ref[0, 0, h] = btx[h][:, hd:] + _bdot_tn(kh[:, sl], v[:, sl])


def _rwkv_scan_kernel(qt_ref, y0_ref, mt_ref, nt_ref, g_ref, bonus_ref, lng_ref, lnb_ref,
                      hsum_ref, hbc_ref, o_ref, st_scr, y_scr):
    hd = B_HEAD_DIM

    @pl.when(pl.program_id(1) == 0)
    def _():
        st_scr[...] = jnp.zeros_like(st_scr)

    for h in range(B_HEADS):
        sl = slice(h * hd, (h + 1) * hd)
        st = st_scr[h]
        y_scr[:, sl] = _bdot(qt_ref[0, :, sl], st) + y0_ref[0, :, sl]
        st_scr[h] = _bdot(mt_ref[0, 0, h], st) + nt_ref[0, 0, h]
    y = y_scr[...]
    hsum = hsum_ref[...]
    hbc = hbc_ref[...]
    per_head_mean = lambda t: _split_dot(_split_dot(t, hsum) * (1.0 / hd), hbc)
    d = y - per_head_mean(y)
    yn = d * lax.rsqrt(per_head_mean(d * d) + GN_EPS)
    o_ref[0] = (((yn * lng_ref[...] + lnb_ref[...]) + bonus_ref[0]) * g_ref[0]).astype(o_ref.dtype)


def _rwkv(zb, mu, w0, w2, a0, a2, g2, k_k, k_a, r_k, lnx_g, lnx_b):
    nb, s, zw = zb.shape
    bw = B_HEADS * B_HEAD_DIM
    ch = _tile(s, RWKV_CHUNK, unit=8)
    nc = s // ch
    head_of_lane = jnp.arange(bw) // B_HEAD_DIM
    hsum = (head_of_lane[:, None] == jnp.arange(LANES)[None, :]).astype(BF16)
    hbc = hsum.T
    tri = (jnp.arange(ch)[:, None] >= jnp.arange(ch)[None, :]).astype(BF16)
    row = lambda t: t.reshape(1, -1).astype(F32)
    const = lambda shape: pl.BlockSpec(shape, lambda b, c: (0,) * len(shape))
    per_chunk = pl.BlockSpec((1, ch, bw), lambda b, c: (b, c, 0))
    per_chunk_mat = pl.BlockSpec((1, 1, B_HEADS, B_HEAD_DIM, B_HEAD_DIM), lambda b, c: (b, c, 0, 0, 0))
    mat_shape = jax.ShapeDtypeStruct((nb, nc, B_HEADS, B_HEAD_DIM, B_HEAD_DIM), F32)
    vec_shape = jax.ShapeDtypeStruct((nb, s, bw), F32)
    qt, y0, mt, nt, g, bonus = pl.pallas_call(
        _rwkv_pre_kernel,
        out_shape=[vec_shape, vec_shape, mat_shape, mat_shape, vec_shape, vec_shape],
        grid=(nb, nc),
        in_specs=[pl.BlockSpec((1, ch, zw), lambda b, c: (b, c, 0)),
                  pl.BlockSpec((1, 8, zw), lambda b, c: (b, jnp.maximum(c * (ch // 8) - 1, 0), 0)),
                  const((1, zw)), const((1, bw)), const(w2.shape), const((1, bw)), const(a2.shape),
                  const(g2.shape), const((1, bw)), const((1, bw)), const((1, bw)),
                  const(hsum.shape), const(hbc.shape), const(tri.shape)],
        out_specs=[per_chunk, per_chunk, per_chunk_mat, per_chunk_mat, per_chunk, per_chunk],
        compiler_params=_cparams(("parallel", "parallel")),
        name="rwkv7_chunk_prepare",
    )(zb, zb, row(mu), row(w0), w2.astype(BF16), row(a0), a2.astype(BF16), g2.astype(BF16),
      row(k_k), row(k_a), row(r_k), hsum, hbc, tri)
    return pl.pallas_call(
        _rwkv_scan_kernel,
        out_shape=jax.ShapeDtypeStruct((nb, s, bw), BF16),
        grid=(nb, nc),
        in_specs=[per_chunk, per_chunk, per_chunk_mat, per_chunk_mat, per_chunk, per_chunk,
                  const((1, bw)), const((1, bw)), const(hsum.shape), const(hbc.shape)],
        out_specs=per_chunk,
        scratch_shapes=[pltpu.VMEM((B_HEADS, B_HEAD_DIM, B_HEAD_DIM), F32),
                        pltpu.VMEM((ch, bw), F32)],
        compiler_params=_cparams(("parallel", "arbitrary")),
        name="rwkv7_chunk_scan",
    )(qt, y0, mt, nt, g, bonus, row(lnx_g), row(lnx_b), hsum, hbc)


def _pad_rows(w, n):
    return jnp.concatenate([w, jnp.zeros((n - w.shape[0],) + w.shape[1:], w.dtype)], axis=0)


def _pad_cols(w, n):
    return jnp.concatenate([w, jnp.zeros(w.shape[:-1] + (n - w.shape[-1],), w.dtype)], axis=-1)


def _even_rwkv(x, scale, shift, w_in, mu, w0, w2, a0, a2, g2, k_k, k_a, r_k, lnx_g, lnx_b):
    bw = B_HEADS * B_HEAD_DIM
    a_in = (A_HEADS + 2 * A_KV_HEADS) * A_HEAD_DIM
    lora_d, lora_a = w2.shape[0], a2.shape[0]
    wb = w_in[:, a_in:]
    o = 3 * bw
    pack = lambda t: jnp.concatenate(
        [t[..., :o], _pad_cols(t[..., o:o + lora_d], LORA_PAD),
         _pad_cols(t[..., o + lora_d:o + lora_d + lora_a], LORA_PAD), t[..., o + lora_d + lora_a:]], axis=-1)
    wbp = pack(wb).astype(BF16)
    zb = _project(x, scale, shift, wbp, out_dtype=F32, tn=512)
    return _rwkv(zb, pack(mu), w0, _pad_rows(w2, LORA_PAD), a0, _pad_rows(a2, LORA_PAD), g2,
                 k_k, k_a, r_k.reshape(-1), lnx_g, lnx_b)


def _even_swa(x, scale, shift, positions, w_in, sinks):
    aq = A_HEADS * A_HEAD_DIM
    akv = A_KV_HEADS * A_HEAD_DIM
    wq = w_in[:, :aq]
    wk = w_in[:, aq:aq + akv]
    wv = w_in[:, aq + akv:aq + 2 * akv]
    dup = lambda w: jnp.concatenate(
        [w[:, h * A_HEAD_DIM:(h + 1) * A_HEAD_DIM] for h in range(A_KV_HEADS) for _ in range(2)], axis=1)
    cos, sin = _rot_tables(positions, A_HEAD_DIM // 2)
    w_qk = jnp.concatenate([wq, dup(wk)], axis=1).astype(BF16)
    qk = _project(x, scale, shift, w_qk, out_dtype=BF16, tn=w_qk.shape[1],
                  rot=(A_HEAD_DIM // 2, cos, sin))
    vdup = _project(x, scale, shift, dup(wv).astype(BF16), out_dtype=BF16, tn=2 * akv)
    return _swa(qk, vdup, sinks)


DSA_TQ = 128
INT_MIN = -2 ** 31


def _sortable(v):
    bits = pltpu.bitcast(v, jnp.int32)
    return jnp.where(bits < 0, bits ^ jnp.int32(0x7FFFFFFF), bits)


def _dsa_index_kernel(qi_ref, ki_ref, wi_ref, o_ref, key_scr, qs_scr, wb_scr, *, n_sel, tk, tc):
    i = pl.program_id(1)
    tq = qi_ref.shape[1]
    s_len = ki_ref.shape[1]
    n_kt = ((i + 1) * tq + tk - 1) // tk
    lane = lax.broadcasted_iota(jnp.int32, (tq, LANES), 1)
    lo = lane < IDX_DIM
    qi = qi_ref[0]
    zero = jnp.zeros((), qi.dtype)
    w = wi_ref[0] * (IDX_HEADS ** -0.5 * IDX_DIM ** -0.5)
    for j in range(IDX_HEADS // 2):
        qp = qi[:, j * LANES:(j + 1) * LANES]
        qs_scr[(2 * j) * tq:(2 * j + 1) * tq, :] = jnp.where(lo, qp, zero)
        qs_scr[(2 * j + 1) * tq:(2 * j + 2) * tq, :] = jnp.where(lo, zero, qp)
    for h in range(IDX_HEADS):
        wb_scr[h] = jnp.broadcast_to(w[:, h:h + 1], (tq, LANES))
    row = i * tq + lax.broadcasted_iota(jnp.int32, (tq, tk), 0)
    coli = lax.broadcasted_iota(jnp.int32, (tq, tk), 1)

    def score_tile(kt, carry):
        start = pl.multiple_of(kt * tk, tk)
        sc = _dot_nt(qs_scr[...], ki_ref[0, pl.ds(start, tk), :])
        acc = jnp.zeros((tq, tk), F32)
        for h in range(IDX_HEADS):
            wt = jnp.concatenate([wb_scr[h]] * (tk // LANES), axis=1)
            acc = acc + jnp.maximum(sc[h * tq:(h + 1) * tq], 0.0) * wt
        acc = jnp.where(start + coli <= row, acc, -jnp.inf)
        key_scr[:, pl.ds(start, tk)] = _sortable(acc)
        return carry

    lax.fori_loop(0, n_kt, score_tile, 0)

    n_ct = ((i + 1) * tq + tc - 1) // tc

    def pad_tile(kt, carry):
        start = pl.multiple_of(kt * tk, tk)
        key_scr[:, pl.ds(start, tk)] = jnp.full((tq, tk), INT_MIN, jnp.int32)
        return carry

    lax.fori_loop(n_kt, n_ct * (tc // tk), pad_tile, 0)

    def count_ge(cand):
        def body(ct, cnt):
            start = pl.multiple_of(ct * tc, tc)
            ge = (key_scr[:, pl.ds(start, tc)] >= cand).astype(jnp.int32)
            for t in range(tc // LANES):
                cnt = cnt + ge[:, t * LANES:(t + 1) * LANES]
            return cnt
        cnt = lax.fori_loop(0, n_ct, body, jnp.zeros((tq, LANES), jnp.int32))
        return jnp.sum(cnt, axis=1, keepdims=True)

    thr = jnp.where(count_ge(jnp.zeros((tq, 1), jnp.int32)) >= n_sel, 0, INT_MIN).astype(jnp.int32)

    def bit_step(it, thr):
        cand = thr | (jnp.int32(1) << (30 - it))
        return jnp.where(count_ge(cand) >= n_sel, cand, thr)

    thr = lax.fori_loop(0, 31, bit_step, thr)

    def emit(kt, carry):
        start = pl.multiple_of(kt * tk, tk)
        sel = (key_scr[:, pl.ds(start, tk)] >= thr) & (start + coli <= row)
        o_ref[0, :, pl.ds(start, tk)] = jnp.where(sel, 0.0, NEG_BIG).astype(o_ref.dtype)
        return carry

    lax.fori_loop(0, n_kt, emit, 0)

    def fill(kt, carry):
        start = pl.multiple_of(kt * tk, tk)
        o_ref[0, :, pl.ds(start, tk)] = jnp.full((tq, tk), NEG_BIG, o_ref.dtype)
        return carry

    lax.fori_loop(n_kt, s_len // tk, fill, 0)


def _dsa_attn_kernel(q_ref, k_ref, v_ref, b_ref, o_ref, sa_scr, sb_scr, *, tk):
    i = pl.program_id(2)
    tq = q_ref.shape[1]
    group = C_HEADS // C_KV_HEADS
    last_tile = k_ref.shape[1] // tk - 1
    n_pair = (((i + 1) * tq + tk - 1) // tk + 1) // 2
    q = q_ref[0]
    qs = jnp.concatenate([q[:, g * LANES:(g + 1) * LANES] for g in range(group)], axis=0)
    rows = group * tq

    def scores(kt):
        start = pl.multiple_of(kt * tk, tk)
        bias = b_ref[0, :, pl.ds(start, tk)].astype(F32)
        return _dot_nt(qs, k_ref[0, pl.ds(start, tk), :]) + jnp.concatenate([bias] * group, axis=0)

    def update(s, kt, m, acc):
        start = pl.multiple_of(kt * tk, tk)
        m_new = jnp.maximum(m, jnp.max(s, axis=1, keepdims=True))
        p = jnp.exp2(s - m_new).astype(BF16)
        pv = _dot(p, v_ref[0, pl.ds(start, tk), :])
        return m_new, jnp.exp2(m - m_new) * acc + pv

    sa_scr[...] = scores(0)

    def body(j, carry):
        m, acc = carry
        sb_scr[...] = scores(2 * j + 1)
        m, acc = update(sa_scr[...], 2 * j, m, acc)
        sa_scr[...] = scores(jnp.minimum(2 * j + 2, last_tile))
        return update(sb_scr[...], 2 * j + 1, m, acc)

    init = (jnp.full((rows, 1), NEG_BIG, F32), jnp.zeros((rows, 2 * C_HEAD_DIM), F32))
    _, acc = lax.fori_loop(0, n_pair, body, init)
    out = acc[:, :C_HEAD_DIM] / acc[:, C_HEAD_DIM:]
    for g in range(group):
        o_ref[0, :, g * LANES:(g + 1) * LANES] = out[g * tq:(g + 1) * tq].astype(o_ref.dtype)


def _dsa(qk, qik, vw):
    nb, s, _ = qk.shape
    n_sel = min(TOPK_MAX, s // 4)
    tq = _tile(s, DSA_TQ)
    cq = C_HEADS * C_HEAD_DIM
    ckv = C_KV_HEADS * C_HEAD_DIM
    iq = IDX_HEADS * IDX_DIM
    tk_i = _tile(s, 256)
    bias = pl.pallas_call(
        functools.partial(_dsa_index_kernel, n_sel=n_sel, tk=tk_i, tc=_tile(s, 1024)),
        out_shape=jax.ShapeDtypeStruct((nb, s, s), BF16),
        grid=(nb, s // tq),
        in_specs=[pl.BlockSpec((1, tq, iq), lambda b, i: (b, i, 0)),
                  pl.BlockSpec((1, s, LANES), lambda b, i: (b, 0, iq // LANES)),
                  pl.BlockSpec((1, tq, LANES), lambda b, i: (b, i, ckv // LANES))],
        out_specs=pl.BlockSpec((1, tq, s), lambda b, i: (b, i, 0)),
        scratch_shapes=[pltpu.VMEM((tq, s), jnp.int32),
                        pltpu.VMEM((IDX_HEADS * tq, LANES), BF16),
                        pltpu.VMEM((IDX_HEADS, tq, LANES), F32)],
        compiler_params=_cparams(("parallel", "parallel")),
        name="dsa_indexer_topk",
    )(qik, qik, vw)
    ones = jnp.ones((nb, s, C_HEAD_DIM), BF16)
    v = jnp.concatenate([t for h in range(C_KV_HEADS)
                         for t in (vw[..., h * C_HEAD_DIM:(h + 1) * C_HEAD_DIM].astype(BF16), ones)], axis=-1)
    tk_a = _tile(s // 2, 512)
    assert (s // tk_a) % 2 == 0
    gw = cq // C_KV_HEADS
    return pl.pallas_call(
        functools.partial(_dsa_attn_kernel, tk=tk_a),
        out_shape=jax.ShapeDtypeStruct((nb, s, cq), BF16),
        grid=(nb, C_KV_HEADS, s // tq),
        in_specs=[pl.BlockSpec((1, tq, gw), lambda b, h, i: (b, i, h)),
                  pl.BlockSpec((1, s, C_HEAD_DIM), lambda b, h, i: (b, 0, cq // C_HEAD_DIM + h)),
                  pl.BlockSpec((1, s, 2 * C_HEAD_DIM), lambda b, h, i: (b, 0, h)),
                  pl.BlockSpec((1, tq, s), lambda b, h, i: (b, i, 0))],
        out_specs=pl.BlockSpec((1, tq, gw), lambda b, h, i: (b, i, h)),
        scratch_shapes=[pltpu.VMEM((gw // C_HEAD_DIM * tq, tk_a), F32)] * 2,
        compiler_params=_cparams(("parallel", "parallel", "parallel")),
        name="dsa_masked_attention",
    )(qk, qk, v, bias)


def _odd_mixer(x, scale, shift, positions, w_in):
    cq = C_HEADS * C_HEAD_DIM
    ckv = C_KV_HEADS * C_HEAD_DIM
    iq = IDX_HEADS * IDX_DIM
    o1, o2, o3, o4 = cq + ckv, cq + 2 * ckv, cq + 2 * ckv + iq, cq + 2 * ckv + iq + IDX_DIM
    q_scale = C_HEAD_DIM ** -0.5 * math.log2(math.e)
    w_qk = jnp.concatenate([w_in[:, :cq] * q_scale, w_in[:, cq:o1]], axis=1).astype(BF16)
    w_v = w_in[:, o1:o2]
    w_ki = w_in[:, o3:o4]
    w_qik = jnp.concatenate([w_in[:, o2:o3], w_ki, w_ki], axis=1).astype(BF16)
    w_vw = jnp.concatenate([w_v, _pad_cols(w_in[:, o4:], LANES)], axis=1).astype(BF16)
    cos64, sin64 = _rot_tables(positions, C_HEAD_DIM // 2)
    cos32, sin32 = _rot_tables(positions, IDX_DIM // 2)
    qk = _project(x, scale, shift, w_qk, out_dtype=BF16, tn=1280, rot=(C_HEAD_DIM // 2, cos64, sin64))
    qik = _project(x, scale, shift, w_qik, out_dtype=BF16, tn=w_qik.shape[1], rot=(IDX_DIM // 2, cos32, sin32))
    vw = _project(x, scale, shift, w_vw, out_dtype=F32, tn=w_vw.shape[1])
    return _dsa(qk, qik, vw)


PEER_ROUTE_TQ = 256
PEER_TQ = 512
PEER_EC = 512


def _top_values(s, n):
    out = []
    for _ in range(n):
        m = jnp.max(s, axis=0, keepdims=True)
        out.append(m)
        s = jnp.where(s == m, -jnp.inf, s)
    return out


def _peer_route_kernel(x_ref, sc_ref, sh_ref, wq_ref, sk_ref, s0_ref, s1_ref, aux_ref):
    h = (x_ref[0] * (1.0 + sc_ref[0]) + sh_ref[0]).astype(BF16)
    q = _dot(h, wq_ref[...]).astype(BF16)
    for hd in range(PEER_HEADS):
        st = []
        for p in range(2):
            col = (2 * hd + p) * LANES
            st.append(_dot_nt(sk_ref[2 * hd + p], q[:, col:col + LANES]))
        s0_ref[0, hd] = st[0]
        s1_ref[0, hd] = st[1]
        v0 = _top_values(st[0], PEER_TOPK)
        v1 = _top_values(st[1], PEER_TOPK)
        v1all = jnp.concatenate(v1, axis=0)
        cand = jnp.concatenate([v0[a] + v1all for a in range(PEER_TOPK)], axis=0)
        tau = _top_values(cand, PEER_TOPK)[-1]
        best = v0[0] + v1[0]
        z = jnp.sum(jnp.where(cand >= tau, jnp.exp(cand - best), 0.0), axis=0, keepdims=True)
        aux_ref[0, 0, hd:hd + 1, :] = tau
        aux_ref[0, 1, hd:hd + 1, :] = v0[0]
        aux_ref[0, 2, hd:hd + 1, :] = v1[0] + jnp.log(z)


def _gelu(a):
    return 0.5 * a * (1.0 + lax.erf(a * (2.0 ** -0.5)))


def _peer_main_kernel(x_ref, sc_ref, sh_ref, gate_ref, u_ref, vt_ref, s0_ref, s1_ref, aux_ref,
                      g_ref, b_ref, o_ref, ht_scr, acc_scr, e1_scr):
    k = pl.program_id(2)
    ec = u_ref.shape[0]
    rows_per_chunk = ec // PEER_NKEYS

    @pl.when(k == 0)
    def _():
        h = x_ref[0] * (1.0 + sc_ref[0]) + sh_ref[0]
        ht_scr[...] = h.T.astype(BF16)
        acc_scr[...] = jnp.zeros_like(acc_scr)
        for hd in range(PEER_HEADS):
            e1_scr[hd] = jnp.exp(s1_ref[0, hd] - aux_ref[0, 2, hd:hd + 1, :])

    a = _dot(u_ref[...], ht_scr[...])
    parts = []
    for r in range(rows_per_chunk):
        n1 = k * rows_per_chunk + r
        gate = jnp.zeros((PEER_NKEYS, a.shape[1]), F32)
        for hd in range(PEER_HEADS):
            s0row = s0_ref[0, hd, pl.ds(n1, 1), :]
            sel = (s0row + s1_ref[0, hd]) >= aux_ref[0, 0, hd:hd + 1, :]
            e0row = jnp.exp(s0row - aux_ref[0, 1, hd:hd + 1, :])
            gate = gate + jnp.where(sel, e1_scr[hd], 0.0) * e0row
        parts.append((_gelu(a[r * PEER_NKEYS:(r + 1) * PEER_NKEYS]) * gate).astype(BF16))
    acc_scr[...] += _dot(vt_ref[...], jnp.concatenate(parts, axis=0))

    @pl.when(k == pl.num_programs(2) - 1)
    def _():
        v = ALPHA * x_ref[0] + (1.0 + gate_ref[0]) * acc_scr[...].T
        o_ref[0] = _layer_norm(v, g_ref[...], b_ref[...])


def _peer_layer(x, scale, shift, gate, wq, subkeys, u_tab, v_tab, ln_g, ln_b):
    nb, s, d = x.shape
    ne = u_tab.shape[0]
    qw = wq.shape[1]
    sk = jnp.transpose(subkeys, (1, 0, 2, 3)).reshape(2 * PEER_HEADS, PEER_NKEYS, -1).astype(BF16)
    tq = _tile(s, PEER_ROUTE_TQ)
    sshape = jax.ShapeDtypeStruct((nb, PEER_HEADS, PEER_NKEYS, s), F32)
    sspec = lambda t: pl.BlockSpec((1, PEER_HEADS, PEER_NKEYS, t), lambda b, i, *_: (b, 0, 0, i))
    aspec = lambda t: pl.BlockSpec((1, 3, PEER_HEADS, t), lambda b, i, *_: (b, 0, 0, i))
    mod_spec = pl.BlockSpec((1, 1, d), lambda b, i, *_: (b, 0, 0))
    s0, s1, aux = pl.pallas_call(
        _peer_route_kernel,
        out_shape=[sshape, sshape, jax.ShapeDtypeStruct((nb, 3, PEER_HEADS, s), F32)],
        grid=(nb, s // tq),
        in_specs=[pl.BlockSpec((1, tq, d), lambda b, i: (b, i, 0)), mod_spec, mod_spec,
                  pl.BlockSpec((d, qw), lambda b, i: (0, 0)),
                  pl.BlockSpec(sk.shape, lambda b, i: (0, 0, 0))],
        out_specs=[sspec(tq), sspec(tq), aspec(tq)],
        compiler_params=_cparams(("parallel", "parallel")),
        name="peer_router",
    )(x, scale, shift, wq.astype(BF16), sk)
    tq = _tile(s, PEER_TQ)
    ec = _tile(ne, PEER_EC)
    return pl.pallas_call(
        _peer_main_kernel,
        out_shape=jax.ShapeDtypeStruct((nb, s, d), F32),
        grid=(nb, s // tq, ne // ec),
        in_specs=[pl.BlockSpec((1, tq, d), lambda b, i, k: (b, i, 0)), mod_spec, mod_spec, mod_spec,
                  pl.BlockSpec((ec, d), lambda b, i, k: (k, 0)),
                  pl.BlockSpec((d, ec), lambda b, i, k: (0, k)),
                  sspec(tq), sspec(tq), aspec(tq),
                  pl.BlockSpec((1, d), lambda b, i, k: (0, 0)),
                  pl.BlockSpec((1, d), lambda b, i, k: (0, 0))],
        out_specs=pl.BlockSpec((1, tq, d), lambda b, i, k: (b, i, 0)),
        scratch_shapes=[pltpu.VMEM((d, tq), BF16), pltpu.VMEM((d, tq), F32),
                        pltpu.VMEM((PEER_HEADS, PEER_NKEYS, tq), F32)],
        compiler_params=_cparams(("parallel", "parallel", "arbitrary")),
        name="peer_experts",
    )(x, scale, shift, gate, u_tab.astype(BF16), v_tab.T.astype(BF16), s0, s1, aux,
      ln_g.reshape(1, d), ln_b.reshape(1, d))


def _split_mod(m):
    shift, scale, gate = jnp.split(m[:, None, :], 3, axis=-1)
    return shift, scale, gate


def kernel(x, c, positions, ev_w_in, ev_w_out, ev_sinks, ev_mu, ev_w0, ev_w2, ev_a0, ev_a2, ev_g2,
           ev_k_k, ev_k_a, ev_r_k, ev_lnx_g, ev_lnx_b, od_w_in, od_w_out, mod_w, mod_b, ln_g, ln_b,
           peer_wq, peer_subkeys, peer_u, peer_v):
    mods = _modulation(c, mod_w, mod_b)
    aq = A_HEADS * A_HEAD_DIM
    for layer in range(DEPTH):
        j = layer // 2
        shift, scale, gate = _split_mod(mods[2 * layer])
        if layer % 2 == 0:
            ya = _even_swa(x, scale, shift, positions, ev_w_in[j], ev_sinks[j])
            yb = _even_rwkv(x, scale, shift, ev_w_in[j], ev_mu[j], ev_w0[j], ev_w2[j], ev_a0[j],
                            ev_a2[j], ev_g2[j], ev_k_k[j], ev_k_a[j], ev_r_k[j], ev_lnx_g[j], ev_lnx_b[j])
            w_out = ev_w_out[j].astype(BF16)
            x = _outproj_ln([ya, yb], [w_out[:aq], w_out[aq:]], x, gate, ln_g[layer, 0], ln_b[layer, 0])
        else:
            y = _odd_mixer(x, scale, shift, positions, od_w_in[j])
            x = _outproj_ln([y], [od_w_out[j].astype(BF16)], x, gate, ln_g[layer, 0], ln_b[layer, 0])
        shift, scale, gate = _split_mod(mods[2 * layer + 1])
        x = _peer_layer(x, scale, shift, gate, peer_wq[layer], peer_subkeys[layer], peer_u[layer],
                        peer_v[layer], ln_g[layer, 1], ln_b[layer, 1])
    return x
```

```python
import functools
import math

import jax
import jax.numpy as jnp
from jax import lax
from jax.experimental import pallas as pl
from jax.experimental.pallas import tpu as pltpu

F32 = jnp.float32
BF16 = jnp.bfloat16

LANES = 128
DEPTH = 2
ALPHA = (2 * DEPTH) ** 0.25
LN_EPS = 1e-5
ROPE_THETA = 10000.0
BLOCK = 128
A_HEADS, A_KV_HEADS, A_HEAD_DIM = 16, 2, 64
B_HEADS, B_HEAD_DIM = 16, 64
GN_EPS = 64e-5
LORA_PAD = 128
C_HEADS, C_KV_HEADS, C_HEAD_DIM = 16, 4, 128
IDX_HEADS, IDX_DIM = 16, 64
TOPK_MAX = 256
PEER_HEADS, PEER_NKEYS, PEER_TOPK = 8, 128, 16
NEG_BIG = -1e30
VMEM_LIMIT = 56 * 1024 * 1024


def _cparams(sem):
    return pltpu.CompilerParams(dimension_semantics=sem, vmem_limit_bytes=VMEM_LIMIT)


def _tile(n, pref, unit=LANES):
    if n <= pref:
        return n
    t = (pref // unit) * unit
    while n % t:
        t -= unit
    assert t > 0, (n, pref)
    return t


def _dot(a, b):
    return jnp.dot(a, b, preferred_element_type=F32)


def _dot_nt(a, b):
    return lax.dot_general(a, b, (((1,), (1,)), ((), ())), preferred_element_type=F32)


def _dot_tn(a, b):
    return lax.dot_general(a, b, (((0,), (0,)), ((), ())), preferred_element_type=F32)


def _split_dot(a, b_bf16):
    hi = a.astype(BF16)
    lo = (a - hi.astype(F32)).astype(BF16)
    return _dot(hi, b_bf16) + _dot(lo, b_bf16)


def _layer_norm(v, g, b):
    mu = jnp.mean(v, axis=-1, keepdims=True)
    d = v - mu
    var = jnp.mean(d * d, axis=-1, keepdims=True)
    return d * lax.rsqrt(var + LN_EPS) * g + b


def _mod_kernel(c_ref, w_ref, b_ref, o_ref):
    c = c_ref[...]
    ca = c * (1.0 / (1.0 + jnp.exp(-c)))
    o_ref[0] = _split_dot(ca, w_ref[0].astype(BF16)) + _split_dot(
        ca, (w_ref[0] - w_ref[0].astype(BF16).astype(F32)).astype(BF16)) + b_ref[0]


def _modulation(c, mod_w, mod_b):
    nb, d = c.shape
    nl = mod_w.shape[0] * mod_w.shape[1]
    n3 = mod_w.shape[-1]
    rows = 8
    cp = jnp.zeros((rows, d), F32).at[:nb].set(c)
    w = mod_w.reshape(nl, d, n3)
    b = mod_b.reshape(nl, 1, n3)
    tn = _tile(n3, 512)
    out = pl.pallas_call(
        _mod_kernel,
        out_shape=jax.ShapeDtypeStruct((nl, rows, n3), F32),
        grid=(nl, n3 // tn),
        in_specs=[pl.BlockSpec((rows, d), lambda l, j: (0, 0)),
                  pl.BlockSpec((1, d, tn), lambda l, j: (l, 0, j)),
                  pl.BlockSpec((1, 1, tn), lambda l, j: (l, 0, j))],
        out_specs=pl.BlockSpec((1, rows, tn), lambda l, j: (l, 0, j)),
        compiler_params=_cparams(("parallel", "parallel")),
        name="modulation",
    )(cp, w, b)
    return out[:, :nb]


def _proj_kernel(*refs, rot_half):
    if rot_half:
        x_ref, sc_ref, sh_ref, w_ref, cos_ref, sin_ref, o_ref, h_scr = refs
    else:
        x_ref, sc_ref, sh_ref, w_ref, o_ref, h_scr = refs

    @pl.when(pl.program_id(2) == 0)
    def _():
        h_scr[...] = (x_ref[0] * (1.0 + sc_ref[0]) + sh_ref[0]).astype(BF16)

    z = _dot(h_scr[...], w_ref[...])
    if rot_half:
        tn = z.shape[1]
        lane = lax.broadcasted_iota(jnp.int32, z.shape, 1)
        first = (lane % (2 * rot_half)) < rot_half
        swapped = jnp.where(first, pltpu.roll(z, tn - rot_half, 1), pltpu.roll(z, rot_half, 1))
        reps = tn // LANES
        cos = jnp.concatenate([cos_ref[0]] * reps, axis=1)
        sin = jnp.concatenate([sin_ref[0]] * reps, axis=1)
        z = z * cos + swapped * sin
    o_ref[0] = z.astype(o_ref.dtype)


def _project(x, scale, shift, w, *, out_dtype, tn, rot=None, tm=512):
    nb, s, d = x.shape
    n = w.shape[1]
    tm = _tile(s, tm)
    tn = _tile(n, tn)
    in_specs = [pl.BlockSpec((1, tm, d), lambda b, i, j: (b, i, 0)),
                pl.BlockSpec((1, 1, d), lambda b, i, j: (b, 0, 0)),
                pl.BlockSpec((1, 1, d), lambda b, i, j: (b, 0, 0)),
                pl.BlockSpec((d, tn), lambda b, i, j: (0, j))]
    args = [x, scale, shift, w]
    rot_half = 0
    if rot is not None:
        rot_half, cos, sin = rot
        in_specs += [pl.BlockSpec((1, tm, LANES), lambda b, i, j: (b, i, 0))] * 2
        args += [cos, sin]
    return pl.pallas_call(
        functools.partial(_proj_kernel, rot_half=rot_half),
        out_shape=jax.ShapeDtypeStruct((nb, s, n), out_dtype),
        grid=(nb, s // tm, n // tn),
        in_specs=in_specs,
        out_specs=pl.BlockSpec((1, tm, tn), lambda b, i, j: (b, i, j)),
        scratch_shapes=[pltpu.VMEM((tm, d), BF16)],
        compiler_params=_cparams(("parallel", "parallel", "arbitrary")),
        name="in_proj_rot%d" % rot_half,
    )(*args)


def _rot_tables(positions, rot_half):
    inv = ROPE_THETA ** (-jnp.arange(rot_half, dtype=F32) * (2.0 / (2 * rot_half)))
    ang = positions.astype(F32)[..., None] * inv
    cos, sin = jnp.cos(ang), jnp.sin(ang)
    reps = LANES // (2 * rot_half)
    cos_t = jnp.concatenate([cos, cos] * reps, axis=-1)
    sin_t = jnp.concatenate([-sin, sin] * reps, axis=-1)
    return cos_t, sin_t


def _outproj_kernel(*refs, n_in):
    ys = refs[:n_in]
    ws = refs[n_in:2 * n_in]
    x_ref, gate_ref, g_ref, b_ref, o_ref = refs[2 * n_in:]
    y = _dot(ys[0][0], ws[0][...])
    for y_ref, w_ref in zip(ys[1:], ws[1:]):
        y = y + _dot(y_ref[0], w_ref[...])
    v = ALPHA * x_ref[0] + (1.0 + gate_ref[0]) * y
    o_ref[0] = _layer_norm(v, g_ref[...], b_ref[...])


def _outproj_ln(ys, ws, x, gate, ln_g, ln_b, *, tm=256):
    nb, s, d = x.shape
    tm = _tile(s, tm)
    n_in = len(ys)
    in_specs = ([pl.BlockSpec((1, tm, y.shape[2]), lambda b, i: (b, i, 0)) for y in ys]
                + [pl.BlockSpec(w.shape, lambda b, i: (0, 0)) for w in ws]
                + [pl.BlockSpec((1, tm, d), lambda b, i: (b, i, 0)),
                   pl.BlockSpec((1, 1, d), lambda b, i: (b, 0, 0)),
                   pl.BlockSpec((1, d), lambda b, i: (0, 0)),
                   pl.BlockSpec((1, d), lambda b, i: (0, 0))])
    return pl.pallas_call(
        functools.partial(_outproj_kernel, n_in=n_in),
        out_shape=jax.ShapeDtypeStruct((nb, s, d), F32),
        grid=(nb, s // tm),
        in_specs=in_specs,
        out_specs=pl.BlockSpec((1, tm, d), lambda b, i: (b, i, 0)),
        compiler_params=_cparams(("parallel", "parallel")),
        name="out_proj_ln",
    )(*ys, *ws, x, gate, ln_g.reshape(1, d), ln_b.reshape(1, d))


def _swa_kernel(sink_ref, q_ref, kp_ref, kc_ref, vp_ref, vc_ref, o_ref):
    i = pl.program_id(1)
    q = q_ref[0]
    kcat = jnp.concatenate([kp_ref[0], kc_ref[0]], axis=0)
    vcat = jnp.concatenate([vp_ref[0], vc_ref[0]], axis=0)
    lane = lax.broadcasted_iota(jnp.int32, (BLOCK, LANES), 1)
    lo = lane < A_HEAD_DIM
    lane2 = lax.broadcasted_iota(jnp.int32, (2 * BLOCK, LANES), 1)
    lo2 = lane2 < A_HEAD_DIM
    row = lax.broadcasted_iota(jnp.int32, (2 * BLOCK, 2 * BLOCK), 0) % BLOCK
    col = lax.broadcasted_iota(jnp.int32, (2 * BLOCK, 2 * BLOCK), 1)
    dist = row + BLOCK - col
    mask = (dist >= 0) & (dist < BLOCK) & ((col >= BLOCK) | (i > 0))
    upper = lax.broadcasted_iota(jnp.int32, (2 * BLOCK, 1), 0) >= BLOCK
    group = A_HEADS // A_KV_HEADS
    zero = jnp.zeros((), q.dtype)
    for j in range(A_HEADS // 2):
        g = (2 * j) // group
        k2 = kcat[:, g * LANES:(g + 1) * LANES]
        v2 = vcat[:, g * LANES:(g + 1) * LANES]
        qp = q[:, j * LANES:(j + 1) * LANES]
        qs = jnp.concatenate([jnp.where(lo, qp, zero), jnp.where(lo, zero, qp)], axis=0)
        s = _dot_nt(qs, k2) * (A_HEAD_DIM ** -0.5)
        s = jnp.where(mask, s, NEG_BIG)
        sink = jnp.where(upper, sink_ref[2 * j + 1], sink_ref[2 * j])
        m = jnp.maximum(jnp.max(s, axis=1, keepdims=True), sink)
        p = jnp.exp(s - m)
        l = jnp.sum(p, axis=1, keepdims=True) + jnp.exp(sink - m)
        p = (p / l).astype(q.dtype)
        o = (_dot(p[:BLOCK], jnp.where(lo2, v2, zero))
             + _dot(p[BLOCK:], jnp.where(lo2, zero, v2)))
        o_ref[0, :, j * LANES:(j + 1) * LANES] = o.astype(o_ref.dtype)


def _swa(qk, vdup, sinks):
    nb, s, _ = qk.shape
    nblk = s // BLOCK
    aq = A_HEADS * A_HEAD_DIM
    kw = 2 * A_KV_HEADS * A_HEAD_DIM
    kcol = aq // kw
    prev = lambda b, i: (b, jnp.maximum(i - 1, 0), 0)
    return pl.pallas_call(
        _swa_kernel,
        out_shape=jax.ShapeDtypeStruct((nb, s, aq), BF16),
        grid=(nb, nblk),
        in_specs=[pl.BlockSpec(memory_space=pltpu.SMEM),
                  pl.BlockSpec((1, BLOCK, aq), lambda b, i: (b, i, 0)),
                  pl.BlockSpec((1, BLOCK, kw), lambda b, i: (b, jnp.maximum(i - 1, 0), kcol)),
                  pl.BlockSpec((1, BLOCK, kw), lambda b, i: (b, i, kcol)),
                  pl.BlockSpec((1, BLOCK, kw), prev),
                  pl.BlockSpec((1, BLOCK, kw), lambda b, i: (b, i, 0))],
        out_specs=pl.BlockSpec((1, BLOCK, aq), lambda b, i: (b, i, 0)),
        compiler_params=_cparams(("parallel", "parallel")),
        name="swa_sinks",
    )(sinks.astype(F32), qk, qk, qk, vdup, vdup)


RWKV_CHUNK = 64


def _bdot(a, b):
    return _dot(a.astype(BF16), b.astype(BF16))


def _bdot_nt(a, b):
    return _dot_nt(a.astype(BF16), b.astype(BF16))


def _bdot_tn(a, b):
    return _dot_tn(a.astype(BF16), b.astype(BF16))


def _sigmoid(x):
    return 1.0 / (1.0 + jnp.exp(-x))


def _rwkv_pre_kernel(z_ref, zp_ref, mu_ref, w0_ref, w2_ref, a0_ref, a2_ref, g2_ref, kk_ref, ka_ref,
                     rk_ref, hsum_ref, hbc_ref, tri_ref,
                     qt_ref, y0_ref, mt_ref, nt_ref, g_ref, bonus_ref):
    c = pl.program_id(1)
    bw = B_HEADS * B_HEAD_DIM
    hd = B_HEAD_DIM
    z = z_ref[0]
    ch = z.shape[0]
    prev = jnp.where(c == 0, 0.0, zp_ref[0][7:8, :])
    rowi = lax.broadcasted_iota(jnp.int32, z.shape, 0)
    zs = jnp.where(rowi == 0, prev, pltpu.roll(z, 1, 0))
    zm = z + (zs - z) * mu_ref[...]
    r = zm[:, 0:bw]
    k = zm[:, bw:2 * bw]
    v = zm[:, 2 * bw:3 * bw]
    wd = zm[:, 3 * bw:3 * bw + LORA_PAD]
    ad = zm[:, 3 * bw + LORA_PAD:3 * bw + 2 * LORA_PAD]
    gd = zm[:, 3 * bw + 2 * LORA_PAD:]

    xw = -(w0_ref[...] + _bdot(jnp.tanh(wd), w2_ref[...]))
    softplus = jnp.maximum(xw, 0.0) + jnp.log(1.0 + jnp.exp(-jnp.abs(xw)))
    lw = -jnp.exp(-softplus - 0.5)
    a = _sigmoid(a0_ref[...] + _bdot(ad, a2_ref[...]))
    g_ref[0] = _bdot(_sigmoid(gd), g2_ref[...])

    hsum = hsum_ref[...]
    hbc = hbc_ref[...]
    per_head = lambda t: _split_dot(_split_dot(t, hsum), hbc)
    kkr = k * kk_ref[...]
    inv = 1.0 / jnp.maximum(jnp.sqrt(_split_dot(kkr * kkr, hsum)), 1e-12)
    kk = kkr * _split_dot(inv, hbc)
    kp = k * (1.0 + (a - 1.0) * ka_ref[...])
    bonus_ref[0] = per_head(r * kp * rk_ref[...]) * v

    lw_hi = lw.astype(BF16)
    lw_lo = (lw - lw_hi.astype(F32)).astype(BF16)
    cum = _dot(tri_ref[...], lw_hi) + _dot(tri_ref[...], lw_lo)
    cum_end = cum[ch - 1:ch, :]
    e_neg = jnp.exp(-cum)
    e_end = jnp.exp(cum_end - cum)
    abar = -kk * jnp.exp(cum - lw)
    rbar = r * jnp.exp(cum)
    beta = kk * a
    bt = beta * e_neg
    kt = kp * e_neg
    bh = beta * e_end
    kh = kp * e_end
    w_end = jnp.exp(cum_end)

    ri = lax.broadcasted_iota(jnp.int32, (ch, ch), 0)
    ci = lax.broadcasted_iota(jnp.int32, (ch, ch), 1)
    strict = ri > ci
    incl = ri >= ci
    eye = (ri == ci).astype(F32)
    ek = (lax.broadcasted_iota(jnp.int32, (hd, hd), 0) == lax.broadcasted_iota(jnp.int32, (hd, hd), 1))
    heads = range(B_HEADS)
    sls = [slice(h * hd, (h + 1) * hd) for h in heads]
    lab = [jnp.where(strict, _bdot_nt(abar[:, sl], bt[:, sl]), 0.0) for sl in sls]
    lak = [jnp.where(strict, _bdot_nt(abar[:, sl], kt[:, sl]), 0.0) for sl in sls]
    rb = [jnp.where(incl, _bdot_nt(rbar[:, sl], bt[:, sl]), 0.0) for sl in sls]
    rk = [jnp.where(incl, _bdot_nt(rbar[:, sl], kt[:, sl]), 0.0) for sl in sls]
    p = [eye + l for l in lab]
    lp = lab
    for _ in range(int(math.log2(ch)) - 1):
        lp = [_bdot(l, l) for l in lp]
        p = [ph + _bdot(ph, l) for ph, l in zip(p, lp)]
    x = [jnp.concatenate([abar[:, sl], _bdot(lk, v[:, sl])], axis=1) for sl, lk in zip(sls, lak)]
    tx = [_bdot(ph, xh) for ph, xh in zip(p, x)]
    rtx = [_bdot(rbh, th) for rbh, th in zip(rb, tx)]
    btx = [_bdot_tn(bh[:, sl], th) for sl, th in zip(sls, tx)]
    for h, sl in zip(heads, sls):
        qt_ref[0, :, sl] = rbar[:, sl] + rtx[h][:, :hd]
        y0_ref[0, :, sl] = rtx[h][:, hd:] + _bdot(rk[h], v[:, sl])
        mt_ref[0, 0, h] = jnp.where(ek, w_end[:, sl], 0.0) + btx[h][:, :hd]
        nt_ref[0, 0, h] = btx[h][:, hd:] + _bdot_tn(kh[:, sl], v[:, sl])


def _rwkv_scan_kernel(qt_ref, y0_ref, mt_ref, nt_ref, g_ref, bonus_ref, lng_ref, lnb_ref,
                      hsum_ref, hbc_ref, o_ref, st_scr, y_scr):
    hd = B_HEAD_DIM

    @pl.when(pl.program_id(1) == 0)
    def _():
        st_scr[...] = jnp.zeros_like(st_scr)

    for h in range(B_HEADS):
        sl = slice(h * hd, (h + 1) * hd)
        st = st_scr[h]
        y_scr[:, sl] = _bdot(qt_ref[0, :, sl], st) + y0_ref[0, :, sl]
        st_scr[h] = _bdot(mt_ref[0, 0, h], st) + nt_ref[0, 0, h]
    y = y_scr[...]
    hsum = hsum_ref[...]
    hbc = hbc_ref[...]
    per_head_mean = lambda t: _split_dot(_split_dot(t, hsum) * (1.0 / hd), hbc)
    d = y - per_head_mean(y)
    yn = d * lax.rsqrt(per_head_mean(d * d) + GN_EPS)
    o_ref[0] = (((yn * lng_ref[...] + lnb_ref[...]) + bonus_ref[0]) * g_ref[0]).astype(o_ref.dtype)


def _rwkv(zb, mu, w0, w2, a0, a2, g2, k_k, k_a, r_k, lnx_g, lnx_b):
    nb, s, zw = zb.shape
    bw = B_HEADS * B_HEAD_DIM
    ch = _tile(s, RWKV_CHUNK, unit=8)
    nc = s // ch
    head_of_lane = jnp.arange(bw) // B_HEAD_DIM
    hsum = (head_of_lane[:, None] == jnp.arange(LANES)[None, :]).astype(BF16)
    hbc = hsum.T
    tri = (jnp.arange(ch)[:, None] >= jnp.arange(ch)[None, :]).astype(BF16)
    row = lambda t: t.reshape(1, -1).astype(F32)
    const = lambda shape: pl.BlockSpec(shape, lambda b, c: (0,) * len(shape))
    per_chunk = pl.BlockSpec((1, ch, bw), lambda b, c: (b, c, 0))
    per_chunk_mat = pl.BlockSpec((1, 1, B_HEADS, B_HEAD_DIM, B_HEAD_DIM), lambda b, c: (b, c, 0, 0, 0))
    mat_shape = jax.ShapeDtypeStruct((nb, nc, B_HEADS, B_HEAD_DIM, B_HEAD_DIM), F32)
    vec_shape = jax.ShapeDtypeStruct((nb, s, bw), F32)
    qt, y0, mt, nt, g, bonus = pl.pallas_call(
        _rwkv_pre_kernel,
        out_shape=[vec_shape, vec_shape, mat_shape, mat_shape, vec_shape, vec_shape],
        grid=(nb, nc),
        in_specs=[pl.BlockSpec((1, ch, zw), lambda b, c: (b, c, 0)),
                  pl.BlockSpec((1, 8, zw), lambda b, c: (b, jnp.maximum(c * (ch // 8) - 1, 0), 0)),
                  const((1, zw)), const((1, bw)), const(w2.shape), const((1, bw)), const(a2.shape),
                  const(g2.shape), const((1, bw)), const((1, bw)), const((1, bw)),
                  const(hsum.shape), const(hbc.shape), const(tri.shape)],
        out_specs=[per_chunk, per_chunk, per_chunk_mat, per_chunk_mat, per_chunk, per_chunk],
        compiler_params=_cparams(("parallel", "parallel")),
        name="rwkv7_chunk_prepare",
    )(zb, zb, row(mu), row(w0), w2.astype(BF16), row(a0), a2.astype(BF16), g2.astype(BF16),
      row(k_k), row(k_a), row(r_k), hsum, hbc, tri)
    return pl.pallas_call(
        _rwkv_scan_kernel,
        out_shape=jax.ShapeDtypeStruct((nb, s, bw), BF16),
        grid=(nb, nc),
        in_specs=[per_chunk, per_chunk, per_chunk_mat, per_chunk_mat, per_chunk, per_chunk,
                  const((1, bw)), const((1, bw)), const(hsum.shape), const(hbc.shape)],
        out_specs=per_chunk,
        scratch_shapes=[pltpu.VMEM((B_HEADS, B_HEAD_DIM, B_HEAD_DIM), F32),
                        pltpu.VMEM((ch, bw), F32)],
        compiler_params=_cparams(("parallel", "arbitrary")),
        name="rwkv7_chunk_scan",
    )(qt, y0, mt, nt, g, bonus, row(lnx_g), row(lnx_b), hsum, hbc)


def _pad_rows(w, n):
    return jnp.concatenate([w, jnp.zeros((n - w.shape[0],) + w.shape[1:], w.dtype)], axis=0)


def _pad_cols(w, n):
    return jnp.concatenate([w, jnp.zeros(w.shape[:-1] + (n - w.shape[-1],), w.dtype)], axis=-1)


def _even_rwkv(x, scale, shift, w_in, mu, w0, w2, a0, a2, g2, k_k, k_a, r_k, lnx_g, lnx_b):
    bw = B_HEADS * B_HEAD_DIM
    a_in = (A_HEADS + 2 * A_KV_HEADS) * A_HEAD_DIM
    lora_d, lora_a = w2.shape[0], a2.shape[0]
    wb = w_in[:, a_in:]
    o = 3 * bw
    pack = lambda t: jnp.concatenate(
        [t[..., :o], _pad_cols(t[..., o:o + lora_d], LORA_PAD),
         _pad_cols(t[..., o + lora_d:o + lora_d + lora_a], LORA_PAD), t[..., o + lora_d + lora_a:]], axis=-1)
    wbp = pack(wb).astype(BF16)
    zb = _project(x, scale, shift, wbp, out_dtype=F32, tn=512)
    return _rwkv(zb, pack(mu), w0, _pad_rows(w2, LORA_PAD), a0, _pad_rows(a2, LORA_PAD), g2,
                 k_k, k_a, r_k.reshape(-1), lnx_g, lnx_b)


def _even_swa(x, scale, shift, positions, w_in, sinks):
    aq = A_HEADS * A_HEAD_DIM
    akv = A_KV_HEADS * A_HEAD_DIM
    wq = w_in[:, :aq]
    wk = w_in[:, aq:aq + akv]
    wv = w_in[:, aq + akv:aq + 2 * akv]
    dup = lambda w: jnp.concatenate(
        [w[:, h * A_HEAD_DIM:(h + 1) * A_HEAD_DIM] for h in range(A_KV_HEADS) for _ in range(2)], axis=1)
    cos, sin = _rot_tables(positions, A_HEAD_DIM // 2)
    w_qk = jnp.concatenate([wq, dup(wk)], axis=1).astype(BF16)
    qk = _project(x, scale, shift, w_qk, out_dtype=BF16, tn=w_qk.shape[1],
                  rot=(A_HEAD_DIM // 2, cos, sin))
    vdup = _project(x, scale, shift, dup(wv).astype(BF16), out_dtype=BF16, tn=2 * akv)
    return _swa(qk, vdup, sinks)


DSA_TQ = 128
INT_MIN = -2 ** 31


def _sortable(v):
    bits = pltpu.bitcast(v, jnp.int32)
    return jnp.where(bits < 0, bits ^ jnp.int32(0x7FFFFFFF), bits)


def _dsa_index_kernel(qi_ref, ki_ref, wi_ref, o_ref, key_scr, qs_scr, wb_scr, sca_scr, scb_scr,
                      *, n_sel, tk, tc):
    i = pl.program_id(1)
    tq = qi_ref.shape[1]
    s_len = ki_ref.shape[1]
    n_kt = ((i + 1) * tq + tk - 1) // tk
    lane = lax.broadcasted_iota(jnp.int32, (tq, LANES), 1)
    lo = lane < IDX_DIM
    qi = qi_ref[0]
    zero = jnp.zeros((), qi.dtype)
    w = wi_ref[0] * (IDX_HEADS ** -0.5 * IDX_DIM ** -0.5)
    for j in range(IDX_HEADS // 2):
        qp = qi[:, j * LANES:(j + 1) * LANES]
        qs_scr[(2 * j) * tq:(2 * j + 1) * tq, :] = jnp.where(lo, qp, zero)
        qs_scr[(2 * j + 1) * tq:(2 * j + 2) * tq, :] = jnp.where(lo, zero, qp)
    for h in range(IDX_HEADS):
        wb_scr[h] = jnp.broadcast_to(w[:, h:h + 1], (tq, LANES))
    row = i * tq + lax.broadcasted_iota(jnp.int32, (tq, tk), 0)
    coli = lax.broadcasted_iota(jnp.int32, (tq, tk), 1)

    last_tile = s_len // tk - 1
    row_l = i * tq + lax.broadcasted_iota(jnp.int32, (tq, LANES), 0)

    def head_scores(kt, sc_ref):
        start = pl.multiple_of(kt * tk, tk)
        sc_ref[...] = _dot_nt(qs_scr[...], ki_ref[0, pl.ds(start, tk), :])

    def reduce_heads(kt, sc_ref):
        start = pl.multiple_of(kt * tk, tk)
        for c in range(tk // LANES):
            cs = slice(c * LANES, (c + 1) * LANES)
            acc = jnp.zeros((tq, LANES), F32)
            for h in range(IDX_HEADS):
                acc = acc + jnp.maximum(sc_ref[h * tq:(h + 1) * tq, cs], 0.0) * wb_scr[h]
            acc = jnp.where(start + c * LANES + lane <= row_l, acc, -jnp.inf)
            key_scr[:, pl.ds(pl.multiple_of(start + c * LANES, LANES), LANES)] = _sortable(acc)

    head_scores(0, sca_scr)

    def score_pair(j, carry):
        head_scores(2 * j + 1, scb_scr)
        reduce_heads(2 * j, sca_scr)
        head_scores(jnp.minimum(2 * j + 2, last_tile), sca_scr)
        reduce_heads(2 * j + 1, scb_scr)
        return carry

    lax.fori_loop(0, (n_kt + 1) // 2, score_pair, 0)

    n_ct = ((i + 1) * tq + tc - 1) // tc

    def pad_tile(kt, carry):
        start = pl.multiple_of(kt * tk, tk)
        key_scr[:, pl.ds(start, tk)] = jnp.full((tq, tk), INT_MIN, jnp.int32)
        return carry

    lax.fori_loop(n_kt, n_ct * (tc // tk), pad_tile, 0)

    def count_ge(cand):
        def body(ct, cnt):
            start = pl.multiple_of(ct * tc, tc)
            ge = (key_scr[:, pl.ds(start, tc)] >= cand).astype(jnp.int32)
            for t in range(tc // LANES):
                cnt = cnt + ge[:, t * LANES:(t + 1) * LANES]
            return cnt
        cnt = lax.fori_loop(0, n_ct, body, jnp.zeros((tq, LANES), jnp.int32))
        return jnp.sum(cnt, axis=1, keepdims=True)

    thr = jnp.where(count_ge(jnp.zeros((tq, 1), jnp.int32)) >= n_sel, 0, INT_MIN).astype(jnp.int32)

    def bit_step(it, thr):
        cand = thr | (jnp.int32(1) << (30 - it))
        return jnp.where(count_ge(cand) >= n_sel, cand, thr)

    thr = lax.fori_loop(0, 31, bit_step, thr)

    def emit(kt, carry):
        start = pl.multiple_of(kt * tk, tk)
        sel = (key_scr[:, pl.ds(start, tk)] >= thr) & (start + coli <= row)
        o_ref[0, :, pl.ds(start, tk)] = jnp.where(sel, 0.0, NEG_BIG).astype(o_ref.dtype)
        return carry

    lax.fori_loop(0, n_kt, emit, 0)

    def fill(kt, carry):
        start = pl.multiple_of(kt * tk, tk)
        o_ref[0, :, pl.ds(start, tk)] = jnp.full((tq, tk), NEG_BIG, o_ref.dtype)
        return carry

    lax.fori_loop(n_kt, s_len // tk, fill, 0)


def _dsa_attn_kernel(q_ref, k_ref, v_ref, b_ref, o_ref, sa_scr, sb_scr, *, tk):
    i = pl.program_id(2)
    tq = q_ref.shape[1]
    group = C_HEADS // C_KV_HEADS
    last_tile = k_ref.shape[1] // tk - 1
    n_pair = (((i + 1) * tq + tk - 1) // tk + 1) // 2
    q = q_ref[0]
    qs = jnp.concatenate([q[:, g * LANES:(g + 1) * LANES] for g in range(group)], axis=0)
    rows = group * tq

    def scores(kt):
        start = pl.multiple_of(kt * tk, tk)
        bias = b_ref[0, :, pl.ds(start, tk)].astype(F32)
        return _dot_nt(qs, k_ref[0, pl.ds(start, tk), :]) + jnp.concatenate([bias] * group, axis=0)

    def update(s, kt, m, acc):
        start = pl.multiple_of(kt * tk, tk)
        m_new = jnp.maximum(m, jnp.max(s, axis=1, keepdims=True))
        p = jnp.exp2(s - m_new).astype(BF16)
        pv = _dot(p, v_ref[0, pl.ds(start, tk), :])
        return m_new, jnp.exp2(m - m_new) * acc + pv

    sa_scr[...] = scores(0)

    def body(j, carry):
        m, acc = carry
        sb_scr[...] = scores(2 * j + 1)
        m, acc = update(sa_scr[...], 2 * j, m, acc)
        sa_scr[...] = scores(jnp.minimum(2 * j + 2, last_tile))
        return update(sb_scr[...], 2 * j + 1, m, acc)

    init = (jnp.full((rows, 1), NEG_BIG, F32), jnp.zeros((rows, 2 * C_HEAD_DIM), F32))
    _, acc = lax.fori_loop(0, n_pair, body, init)
    out = acc[:, :C_HEAD_DIM] / acc[:, C_HEAD_DIM:]
    for g in range(group):
        o_ref[0, :, g * LANES:(g + 1) * LANES] = out[g * tq:(g + 1) * tq].astype(o_ref.dtype)


def _dsa(qk, qik, vw):
    nb, s, _ = qk.shape
    n_sel = min(TOPK_MAX, s // 4)
    tq = _tile(s, DSA_TQ)
    cq = C_HEADS * C_HEAD_DIM
    ckv = C_KV_HEADS * C_HEAD_DIM
    iq = IDX_HEADS * IDX_DIM
    tk_i = _tile(s // 2, 256)
    assert (s // tk_i) % 2 == 0
    bias = pl.pallas_call(
        functools.partial(_dsa_index_kernel, n_sel=n_sel, tk=tk_i, tc=_tile(s, 1024)),
        out_shape=jax.ShapeDtypeStruct((nb, s, s), BF16),
        grid=(nb, s // tq),
        in_specs=[pl.BlockSpec((1, tq, iq), lambda b, i: (b, i, 0)),
                  pl.BlockSpec((1, s, LANES), lambda b, i: (b, 0, iq // LANES)),
                  pl.BlockSpec((1, tq, LANES), lambda b, i: (b, i, ckv // LANES))],
        out_specs=pl.BlockSpec((1, tq, s), lambda b, i: (b, i, 0)),
        scratch_shapes=[pltpu.VMEM((tq, s), jnp.int32),
                        pltpu.VMEM((IDX_HEADS * tq, LANES), BF16),
                        pltpu.VMEM((IDX_HEADS, tq, LANES), F32),
                        pltpu.VMEM((IDX_HEADS * tq, tk_i), F32),
                        pltpu.VMEM((IDX_HEADS * tq, tk_i), F32)],
        compiler_params=_cparams(("parallel", "parallel")),
        name="dsa_indexer_topk",
    )(qik, qik, vw)
    ones = jnp.ones((nb, s, C_HEAD_DIM), BF16)
    v = jnp.concatenate([t for h in range(C_KV_HEADS)
                         for t in (vw[..., h * C_HEAD_DIM:(h + 1) * C_HEAD_DIM].astype(BF16), ones)], axis=-1)
    tk_a = _tile(s // 2, 1024)
    assert (s // tk_a) % 2 == 0
    gw = cq // C_KV_HEADS
    return pl.pallas_call(
        functools.partial(_dsa_attn_kernel, tk=tk_a),
        out_shape=jax.ShapeDtypeStruct((nb, s, cq), BF16),
        grid=(nb, C_KV_HEADS, s // tq),
        in_specs=[pl.BlockSpec((1, tq, gw), lambda b, h, i: (b, i, h)),
                  pl.BlockSpec((1, s, C_HEAD_DIM), lambda b, h, i: (b, 0, cq // C_HEAD_DIM + h)),
                  pl.BlockSpec((1, s, 2 * C_HEAD_DIM), lambda b, h, i: (b, 0, h)),
                  pl.BlockSpec((1, tq, s), lambda b, h, i: (b, i, 0))],
        out_specs=pl.BlockSpec((1, tq, gw), lambda b, h, i: (b, i, h)),
        scratch_shapes=[pltpu.VMEM((gw // C_HEAD_DIM * tq, tk_a), F32)] * 2,
        compiler_params=_cparams(("parallel", "parallel", "parallel")),
        name="dsa_masked_attention",
    )(qk, qk, v, bias)


def _odd_mixer(x, scale, shift, positions, w_in):
    cq = C_HEADS * C_HEAD_DIM
    ckv = C_KV_HEADS * C_HEAD_DIM
    iq = IDX_HEADS * IDX_DIM
    o1, o2, o3, o4 = cq + ckv, cq + 2 * ckv, cq + 2 * ckv + iq, cq + 2 * ckv + iq + IDX_DIM
    q_scale = C_HEAD_DIM ** -0.5 * math.log2(math.e)
    w_qk = jnp.concatenate([w_in[:, :cq] * q_scale, w_in[:, cq:o1]], axis=1).astype(BF16)
    w_v = w_in[:, o1:o2]
    w_ki = w_in[:, o3:o4]
    w_qik = jnp.concatenate([w_in[:, o2:o3], w_ki, w_ki], axis=1).astype(BF16)
    w_vw = jnp.concatenate([w_v, _pad_cols(w_in[:, o4:], LANES)], axis=1).astype(BF16)
    cos64, sin64 = _rot_tables(positions, C_HEAD_DIM // 2)
    cos32, sin32 = _rot_tables(positions, IDX_DIM // 2)
    qk = _project(x, scale, shift, w_qk, out_dtype=BF16, tn=1280, rot=(C_HEAD_DIM // 2, cos64, sin64))
    qik = _project(x, scale, shift, w_qik, out_dtype=BF16, tn=w_qik.shape[1], rot=(IDX_DIM // 2, cos32, sin32))
    vw = _project(x, scale, shift, w_vw, out_dtype=F32, tn=w_vw.shape[1])
    return _dsa(qk, qik, vw)


PEER_ROUTE_TQ = 256
PEER_TQ = 512
PEER_EC = 1024


def _top_values(s, n):
    out = []
    for _ in range(n):
        m = jnp.max(s, axis=0, keepdims=True)
        out.append(m)
        s = jnp.where(s == m, -jnp.inf, s)
    return out


def _peer_route_kernel(x_ref, sc_ref, sh_ref, wq_ref, sk_ref, s0_ref, s1_ref, aux_ref):
    h = (x_ref[0] * (1.0 + sc_ref[0]) + sh_ref[0]).astype(BF16)
    q = _dot(h, wq_ref[...]).astype(BF16)
    for hd in range(PEER_HEADS):
        st = []
        for p in range(2):
            col = (2 * hd + p) * LANES
            st.append(_dot_nt(sk_ref[2 * hd + p], q[:, col:col + LANES]))
        s0_ref[0, hd] = st[0]
        s1_ref[0, hd] = st[1]
        v0 = _top_values(st[0], PEER_TOPK)
        v1 = _top_values(st[1], PEER_TOPK)
        v1all = jnp.concatenate(v1, axis=0)
        cand = jnp.concatenate([v0[a] + v1all for a in range(PEER_TOPK)], axis=0)
        tau = _top_values(cand, PEER_TOPK)[-1]
        best = v0[0] + v1[0]
        z = jnp.sum(jnp.where(cand >= tau, jnp.exp(cand - best), 0.0), axis=0, keepdims=True)
        aux_ref[0, 0, hd:hd + 1, :] = tau
        aux_ref[0, 1, hd:hd + 1, :] = v0[0]
        aux_ref[0, 2, hd:hd + 1, :] = v1[0] + jnp.log(z)


def _gelu(a):
    return 0.5 * a * (1.0 + lax.erf(a * (2.0 ** -0.5)))


def _peer_main_kernel(x_ref, sc_ref, sh_ref, gate_ref, u_ref, vt_ref, s0_ref, s1_ref, aux_ref,
                      g_ref, b_ref, o_ref, ht_scr, acc_scr, e1_scr, e0_scr):
    k = pl.program_id(2)
    ec = u_ref.shape[0]
    rows_per_chunk = ec // PEER_NKEYS

    @pl.when(k == 0)
    def _():
        h = x_ref[0] * (1.0 + sc_ref[0]) + sh_ref[0]
        ht_scr[...] = h.T.astype(BF16)
        acc_scr[...] = jnp.zeros_like(acc_scr)
        for hd in range(PEER_HEADS):
            e1_scr[hd] = jnp.exp(s1_ref[0, hd] - aux_ref[0, 2, hd:hd + 1, :])

    tq = ht_scr.shape[1]
    for hd in range(PEER_HEADS):
        e0_scr[hd] = jnp.exp(s0_ref[0, hd] - aux_ref[0, 1, hd:hd + 1, :])
    half = 2 * LANES
    halves = [slice(t * half, (t + 1) * half) for t in range(tq // half)]
    a = [_dot(u_ref[...], ht_scr[:, hs]) for hs in halves]
    for t, hs in enumerate(halves):
        cols = []
        for c in range(half // LANES):
            cs = slice(t * half + c * LANES, t * half + (c + 1) * LANES)
            rows = []
            for r in range(rows_per_chunk):
                gate = jnp.zeros((PEER_NKEYS, LANES), F32)
                for hd in range(PEER_HEADS):
                    sel = (s0_ref[0, hd, r:r + 1, cs] + s1_ref[0, hd, :, cs]) >= aux_ref[0, 0, hd:hd + 1, cs]
                    gate = gate + jnp.where(sel, e1_scr[hd, :, cs], 0.0) * e0_scr[hd, r:r + 1, cs]
                a_rc = a[t][r * PEER_NKEYS:(r + 1) * PEER_NKEYS, c * LANES:(c + 1) * LANES]
                rows.append((_gelu(a_rc) * gate).astype(BF16))
            cols.append(jnp.concatenate(rows, axis=0))
        acc_scr[:, hs] += _dot(vt_ref[...], jnp.concatenate(cols, axis=1))

    @pl.when(k == pl.num_programs(2) - 1)
    def _():
        v = ALPHA * x_ref[0] + (1.0 + gate_ref[0]) * acc_scr[...].T
        o_ref[0] = _layer_norm(v, g_ref[...], b_ref[...])


def _peer_layer(x, scale, shift, gate, wq, subkeys, u_tab, v_tab, ln_g, ln_b):
    nb, s, d = x.shape
    ne = u_tab.shape[0]
    qw = wq.shape[1]
    sk = jnp.transpose(subkeys, (1, 0, 2, 3)).reshape(2 * PEER_HEADS, PEER_NKEYS, -1).astype(BF16)
    tq = _tile(s, PEER_ROUTE_TQ)
    sshape = jax.ShapeDtypeStruct((nb, PEER_HEADS, PEER_NKEYS, s), F32)
    sspec = lambda t: pl.BlockSpec((1, PEER_HEADS, PEER_NKEYS, t), lambda b, i, *_: (b, 0, 0, i))
    aspec = lambda t: pl.BlockSpec((1, 3, PEER_HEADS, t), lambda b, i, *_: (b, 0, 0, i))
    mod_spec = pl.BlockSpec((1, 1, d), lambda b, i, *_: (b, 0, 0))
    s0, s1, aux = pl.pallas_call(
        _peer_route_kernel,
        out_shape=[sshape, sshape, jax.ShapeDtypeStruct((nb, 3, PEER_HEADS, s), F32)],
        grid=(nb, s // tq),
        in_specs=[pl.BlockSpec((1, tq, d), lambda b, i: (b, i, 0)), mod_spec, mod_spec,
                  pl.BlockSpec((d, qw), lambda b, i: (0, 0)),
                  pl.BlockSpec(sk.shape, lambda b, i: (0, 0, 0))],
        out_specs=[sspec(tq), sspec(tq), aspec(tq)],
        compiler_params=_cparams(("parallel", "parallel")),
        name="peer_router",
    )(x, scale, shift, wq.astype(BF16), sk)
    tq = _tile(s, PEER_TQ)
    ec = _tile(ne, PEER_EC)
    assert (ec // PEER_NKEYS) % 8 == 0
    return pl.pallas_call(
        _peer_main_kernel,
        out_shape=jax.ShapeDtypeStruct((nb, s, d), F32),
        grid=(nb, s // tq, ne // ec),
        in_specs=[pl.BlockSpec((1, tq, d), lambda b, i, k: (b, i, 0)), mod_spec, mod_spec, mod_spec,
                  pl.BlockSpec((ec, d), lambda b, i, k: (k, 0)),
                  pl.BlockSpec((d, ec), lambda b, i, k: (0, k)),
                  pl.BlockSpec((1, PEER_HEADS, ec // PEER_NKEYS, tq), lambda b, i, k: (b, 0, k, i)),
                  sspec(tq), aspec(tq),
                  pl.BlockSpec((1, d), lambda b, i, k: (0, 0)),
                  pl.BlockSpec((1, d), lambda b, i, k: (0, 0))],
        out_specs=pl.BlockSpec((1, tq, d), lambda b, i, k: (b, i, 0)),
        scratch_shapes=[pltpu.VMEM((d, tq), BF16), pltpu.VMEM((d, tq), F32),
                        pltpu.VMEM((PEER_HEADS, PEER_NKEYS, tq), F32),
                        pltpu.VMEM((PEER_HEADS, ec // PEER_NKEYS, tq), F32)],
        compiler_params=_cparams(("parallel", "parallel", "arbitrary")),
        name="peer_experts",
    )(x, scale, shift, gate, u_tab.astype(BF16), v_tab.T.astype(BF16), s0, s1, aux,
      ln_g.reshape(1, d), ln_b.reshape(1, d))


def _split_mod(m):
    shift, scale, gate = jnp.split(m[:, None, :], 3, axis=-1)
    return shift, scale, gate


def kernel(x, c, positions, ev_w_in, ev_w_out, ev_sinks, ev_mu, ev_w0, ev_w2, ev_a0, ev_a2, ev_g2,
           ev_k_k, ev_k_a, ev_r_k, ev_lnx_g, ev_lnx_b, od_w_in, od_w_out, mod_w, mod_b, ln_g, ln_b,
           peer_wq, peer_subkeys, peer_u, peer_v):
    mods = _modulation(c, mod_w, mod_b)
    aq = A_HEADS * A_HEAD_DIM
    for layer in range(DEPTH):
        j = layer // 2
        shift, scale, gate = _split_mod(mods[2 * layer])
        if layer % 2 == 0:
            ya = _even_swa(x, scale, shift, positions, ev_w_in[j], ev_sinks[j])
            yb = _even_rwkv(x, scale, shift, ev_w_in[j], ev_mu[j], ev_w0[j], ev_w2[j], ev_a0[j],
                            ev_a2[j], ev_g2[j], ev_k_k[j], ev_k_a[j], ev_r_k[j], ev_lnx_g[j], ev_lnx_b[j])
            w_out = ev_w_out[j].astype(BF16)
            x = _outproj_ln([ya, yb], [w_out[:aq], w_out[aq:]], x, gate, ln_g[layer, 0], ln_b[layer, 0])
        else:
            y = _odd_mixer(x, scale, shift, positions, od_w_in[j])
            x = _outproj_ln([y], [od_w_out[j].astype(BF16)], x, gate, ln_g[layer, 0], ln_b[layer, 0])
        shift, scale, gate = _split_mod(mods[2 * layer + 1])
        x = _peer_layer(x, scale, shift, gate, peer_wq[layer], peer_subkeys[layer], peer_u[layer],
                        peer_v[layer], ln_g[layer, 1], ln_b[layer, 1])
    return x
```
